```python
import jax, jax.numpy as jnp
from jax import lax
import numpy as np

D_MODEL = 1024
BATCH = 4
SEQ = 8192
DEPTH = 1
DEC_BATCH = 128
DEC_SEQ = 1
PAST_LEN = 8192
PAGE_SIZE = 128

SB_HEADS = 8
SB_DIM = 64
NSA_HEADS = 8
NSA_GROUPS = 2
NSA_DIM = 64
HEADS_PER_GROUP = NSA_HEADS // NSA_GROUPS
CMP_LEN = 32
CMP_STRIDE = 16
CMP_RATIO = CMP_LEN // CMP_STRIDE
SLC_BLOCK = 64
N_SEL = 16
WINDOW = 512
N_NSA_BRANCH = 3
N_MERGE = 2
D_FF = 4 * D_MODEL
Q_BLOCK = 128
EPS = 1e-6
NEG = -1e30
FORCE = 1e4
SB_W = SB_HEADS * SB_DIM
NSA_QW = NSA_HEADS * NSA_DIM
NSA_KVW = NSA_GROUPS * NSA_DIM
IN_SPLITS = (SB_W, SB_W, SB_W, NSA_QW, NSA_KVW, NSA_KVW, NSA_KVW, NSA_KVW, NSA_KVW, NSA_KVW,
             NSA_HEADS * N_NSA_BRANCH, N_MERGE * D_MODEL)
IN_W = sum(IN_SPLITS)

kernel_name = 'stickbreak_nsa_hybrid_step'


def rmsnorm(x, g):
    xf = x.astype(jnp.float32)
    y = xf * lax.rsqrt(jnp.mean(xf * xf, axis=-1, keepdims=True) + EPS)
    return (y * g.astype(jnp.float32)).astype(x.dtype)


def alibi_slopes(n_heads):
    return jnp.asarray(2.0 ** (-8.0 * (np.arange(n_heads) + 1) / n_heads), dtype=jnp.float32)


def masked_softmax(s, mask):
    return jax.nn.softmax(jnp.where(mask, s, NEG), axis=-1) * mask


def cmp_to_slc(n_cmp, n_slc):
    c_start = np.arange(n_cmp) * CMP_STRIDE
    s_start = np.arange(n_slc) * SLC_BLOCK
    inter = (np.minimum(c_start[:, None] + CMP_LEN, s_start[None, :] + SLC_BLOCK)
             - np.maximum(c_start[:, None], s_start[None, :]))
    return jnp.asarray(np.clip(inter, 0, None) / CMP_LEN, dtype=jnp.float32)


def compress(k, pool_w, proj_w):
    b, t = k.shape[:2]
    n_sub = t // CMP_STRIDE
    n_cmp = n_sub - CMP_RATIO + 1
    sub = k[:, :n_sub * CMP_STRIDE].reshape(b, n_sub, CMP_STRIDE, NSA_GROUPS, NSA_DIM)
    pw = pool_w.reshape(CMP_RATIO, CMP_STRIDE, NSA_GROUPS, NSA_DIM)
    pooled = jnp.einsum('bmrgd,rgd->bmgd', sub[:, 0:n_cmp], pw[0])
    for c in range(1, CMP_RATIO):
        pooled = pooled + jnp.einsum('bmrgd,rgd->bmgd', sub[:, c:c + n_cmp], pw[c])
    return jnp.einsum('bmgd,gde->bmge', pooled, proj_w)


def project(x, norm_g, w_in):
    b, s, _ = x.shape
    z = rmsnorm(x, norm_g) @ w_in
    sq, sk, sv, nq, ck, cv, lk, lv, wk, wv, ng, mg = jnp.split(z, np.cumsum(IN_SPLITS)[:-1].tolist(), axis=-1)
    sb = lambda t: t.reshape(b, s, SB_HEADS, SB_DIM)
    kv = lambda t: t.reshape(b, s, NSA_GROUPS, NSA_DIM)
    return (sb(sq), sb(sk), sb(sv), nq.reshape(b, s, NSA_HEADS, NSA_DIM),
            kv(ck), kv(cv), kv(lk), kv(lv), kv(wk), kv(wv),
            ng.reshape(b, s, NSA_HEADS, N_NSA_BRANCH), mg)


def sb_attend(q, k_segs, v_segs, q_pos, k_pos):
    b, nq, h, dh = q.shape
    qs = q * (dh ** -0.5)
    z = jnp.concatenate([jnp.einsum('bqhd,bkhd->bhqk', qs, k, preferred_element_type=jnp.float32)
                         for k in k_segs], axis=-1)
    causal = k_pos[None, :] < q_pos[:, None]
    log_keep = jnp.where(causal, jax.nn.log_sigmoid(-z), 0.0)
    log_pass = lax.cumsum(log_keep, axis=3, reverse=True) - log_keep
    a = jnp.where(causal, jnp.exp(jax.nn.log_sigmoid(z) + log_pass), 0.0)
    bounds = np.cumsum([k.shape[1] for k in k_segs])[:-1].tolist()
    parts = [jnp.einsum('bhqk,bkhd->bqhd', a_i.astype(v.dtype), v, preferred_element_type=jnp.float32)
             for a_i, v in zip(jnp.split(a, bounds, axis=3), v_segs)]
    o = parts[0]
    for p in parts[1:]:
        o = o + p
    return o.reshape(b, nq, h * dh).astype(q.dtype)


def nsa_attend(q, gates, q_pos, kc, vc, c_end, ks, vs, kw, vw, w_pos, ov):
    b, nq = q.shape[:2]
    qg = q.reshape(b, nq, NSA_GROUPS, HEADS_PER_GROUP, NSA_DIM) * (NSA_DIM ** -0.5)
    slope = alibi_slopes(NSA_HEADS).reshape(NSA_GROUPS, HEADS_PER_GROUP)[None, :, :, None, None]
    dist_c = (q_pos[:, None] - c_end[None, :]).astype(jnp.float32)
    s_c = jnp.einsum('bqghd,bcgd->bghqc', qg, kc, preferred_element_type=jnp.float32) - slope * dist_c
    p_c = masked_softmax(s_c, dist_c >= 0)
    o_c = jnp.einsum('bghqc,bcgd->bqghd', p_c.astype(vc.dtype), vc, preferred_element_type=jnp.float32)
    n_slc = ov.shape[1]
    imp = jnp.einsum('bghqc,cn->bgqn', p_c, ov)
    blk = jnp.arange(n_slc)[None, :]
    cur = (q_pos // SLC_BLOCK)[:, None]
    forced = (blk == 0) | (blk == cur) | (blk == cur - 1)
    imp = jnp.where(forced, FORCE, jnp.where(blk <= cur, imp, -FORCE))
    _, idx = lax.top_k(imp, min(N_SEL, n_slc))
    tok = (idx[..., None] * SLC_BLOCK + jnp.arange(SLC_BLOCK)).reshape(b, NSA_GROUPS, nq, -1)
    gather = jax.vmap(jax.vmap(lambda rows, ii: rows[ii]))
    k_sel = gather(jnp.swapaxes(ks, 1, 2), tok)
    v_sel = gather(jnp.swapaxes(vs, 1, 2), tok)
    dist_s = (q_pos[None, None, :, None] - tok).astype(jnp.float32)[:, :, None]
    s_s = jnp.einsum('bqghd,bgqnd->bghqn', qg, k_sel, preferred_element_type=jnp.float32) - slope * dist_s
    p_s = masked_softmax(s_s, dist_s >= 0)
    o_s = jnp.einsum('bghqn,bgqnd->bqghd', p_s.astype(v_sel.dtype), v_sel, preferred_element_type=jnp.float32)
    dist_w = (q_pos[:, None] - w_pos[None, :]).astype(jnp.float32)
    mask_w = (dist_w >= 0) & (dist_w <= WINDOW) & (w_pos[None, :] >= 0)
    s_w = jnp.einsum('bqghd,bkgd->bghqk', qg, kw, preferred_element_type=jnp.float32) - slope * dist_w
    p_w = masked_softmax(s_w, mask_w)
    o_w = jnp.einsum('bghqk,bkgd->bqghd', p_w.astype(vw.dtype), vw, preferred_element_type=jnp.float32)
    g = jax.nn.sigmoid(gates.astype(jnp.float32)).reshape(b, nq, NSA_GROUPS, HEADS_PER_GROUP, N_NSA_BRANCH)
    o = g[..., 0:1] * o_c + g[..., 1:2] * o_s + g[..., 2:3] * o_w
    return o.reshape(b, nq, NSA_QW).astype(q.dtype)


def merge_out(x, o_sb, o_nsa, mg, w_up_sb, w_up_nsa, w_out, norm2_g, w_ff_in, w_ff_out):
    g = jax.nn.sigmoid(mg.astype(jnp.float32)).astype(x.dtype)
    g_sb, g_nsa = jnp.split(g, N_MERGE, axis=-1)
    mixed = g_sb * (o_sb @ w_up_sb) + g_nsa * (o_nsa @ w_up_nsa)
    x = x + mixed @ w_out
    u = jax.nn.relu(rmsnorm(x, norm2_g) @ w_ff_in)
    return x + (u * u) @ w_ff_out


def pad_rows(t, front, back):
    return jnp.pad(t, ((0, 0), (front, back), (0, 0), (0, 0)))


def prompt_layer(x, wl):
    norm1_g, w_in, pool_k, pool_v, proj_k, proj_v, w_up_sb, w_up_nsa, w_out, norm2_g, w_ff_in, w_ff_out = wl
    b, s, _ = x.shape
    sq, sk, sv, nq, ck, cv, lk, lv, wk, wv, ng, mg = project(x, norm1_g, w_in)
    kc = compress(ck, pool_k, proj_k)
    vc = compress(cv, pool_v, proj_v)
    n_cmp = kc.shape[1]
    c_end = jnp.arange(n_cmp) * CMP_STRIDE + CMP_LEN - 1
    n_slc = -(-s // SLC_BLOCK)
    ov = cmp_to_slc(n_cmp, n_slc)
    lk_p = pad_rows(lk, 0, n_slc * SLC_BLOCK - s)
    lv_p = pad_rows(lv, 0, n_slc * SLC_BLOCK - s)
    wk_p = pad_rows(wk, WINDOW, 0)
    wv_p = pad_rows(wv, WINDOW, 0)
    k_pos = jnp.arange(s)

    def block(i):
        q0 = i * Q_BLOCK
        q_pos = q0 + jnp.arange(Q_BLOCK)
        rows = lambda t: lax.dynamic_slice_in_dim(t, q0, Q_BLOCK, axis=1)
        band = lambda t: lax.dynamic_slice_in_dim(t, q0, Q_BLOCK + WINDOW, axis=1)
        o_sb = sb_attend(rows(sq), (sk,), (sv,), q_pos, k_pos)
        w_pos = q0 - WINDOW + jnp.arange(Q_BLOCK + WINDOW)
        o_nsa = nsa_attend(rows(nq), rows(ng), q_pos, kc, vc, c_end, lk_p, lv_p,
                           band(wk_p), band(wv_p), w_pos, ov)
        return o_sb, o_nsa

    o_sb, o_nsa = lax.map(block, jnp.arange(s // Q_BLOCK))
    unblock = lambda o: jnp.moveaxis(o, 0, 1).reshape(b, s, -1)
    y = merge_out(x, unblock(o_sb), unblock(o_nsa), mg, w_up_sb, w_up_nsa, w_out, norm2_g, w_ff_in, w_ff_out)
    n_keep = min(WINDOW, s)
    return y, (sk, sv, ck, cv, lk, lv, wk[:, s - n_keep:], wv[:, s - n_keep:])


def sample_layer(x, l, caches, win_k, win_v, page_table, wl):
    norm1_g, w_in, pool_k, pool_v, proj_k, proj_v, w_up_sb, w_up_nsa, w_out, norm2_g, w_ff_in, w_ff_out = wl
    c_sb_k, c_sb_v, c_cmp_k, c_cmp_v, c_slc_k, c_slc_v = caches
    b, n_new, _ = x.shape
    past = page_table.shape[1] * PAGE_SIZE
    total = past + n_new
    paged = lambda c: c[l, page_table].reshape(b, past, *c.shape[3:])
    sq, sk, sv, nq, ck, cv, lk, lv, wk, wv, ng, mg = project(x, norm1_g, w_in)
    q_pos = past + jnp.arange(n_new)
    o_sb = sb_attend(sq, (paged(c_sb_k), sk), (paged(c_sb_v), sv), q_pos, jnp.arange(total))
    kc = compress(jnp.concatenate([paged(c_cmp_k), ck], axis=1), pool_k, proj_k)
    vc = compress(jnp.concatenate([paged(c_cmp_v), cv], axis=1), pool_v, proj_v)
    n_cmp = kc.shape[1]
    c_end = jnp.arange(n_cmp) * CMP_STRIDE + CMP_LEN - 1
    n_slc = -(-total // SLC_BLOCK)
    ov = cmp_to_slc(n_cmp, n_slc)
    pad = n_slc * SLC_BLOCK - total
    lk_all = pad_rows(jnp.concatenate([paged(c_slc_k), lk], axis=1), 0, pad)
    lv_all = pad_rows(jnp.concatenate([paged(c_slc_v), lv], axis=1), 0, pad)
    n_buf = win_k.shape[1]
    wk_all = jnp.concatenate([win_k, wk], axis=1)
    wv_all = jnp.concatenate([win_v, wv], axis=1)
    w_pos = past - n_buf + jnp.arange(n_buf + n_new)
    o_nsa = nsa_attend(nq, ng, q_pos, kc, vc, c_end, lk_all, lv_all, wk_all, wv_all, w_pos, ov)
    y = merge_out(x, o_sb, o_nsa, mg, w_up_sb, w_up_nsa, w_out, norm2_g, w_ff_in, w_ff_out)
    n_keep = min(WINDOW, total)
    return y, (sk, sv, ck, cv, lk, lv, wk_all[:, n_buf + n_new - n_keep:], wv_all[:, n_buf + n_new - n_keep:])


def setup_inputs(seed: int = 0) -> dict:
    key = jax.random.key(seed)
    ks = jax.random.split(key, 32)
    f32 = jnp.float32
    n_pages = PAST_LEN // PAGE_SIZE
    n_pool = (DEC_BATCH * n_pages * 5) // 4
    n_buf = min(WINDOW, PAST_LEN)
    nrm = lambda k, shape, scale: scale * jax.random.normal(k, shape, f32)
    page_table = jax.random.permutation(ks[0], n_pool)[:DEC_BATCH * n_pages].reshape(DEC_BATCH, n_pages).astype(jnp.int32)
    sb_page = (DEPTH, n_pool, PAGE_SIZE, SB_HEADS, SB_DIM)
    kv_page = (DEPTH, n_pool, PAGE_SIZE, NSA_GROUPS, NSA_DIM)
    win = (DEPTH, DEC_BATCH, n_buf, NSA_GROUPS, NSA_DIM)
    pool_shape = (DEPTH, CMP_LEN, NSA_GROUPS, NSA_DIM)
    return {
        'x_prompt': nrm(ks[1], (BATCH, SEQ, D_MODEL), 1.0),
        'x_sample': nrm(ks[2], (DEC_BATCH, DEC_SEQ, D_MODEL), 1.0),
        'cache_sb_k': nrm(ks[3], sb_page, 1.0),
        'cache_sb_v': nrm(ks[4], sb_page, 1.0),
        'cache_cmp_k': nrm(ks[5], kv_page, 1.0),
        'cache_cmp_v': nrm(ks[6], kv_page, 1.0),
        'cache_slc_k': nrm(ks[7], kv_page, 1.0),
        'cache_slc_v': nrm(ks[8], kv_page, 1.0),
        'state_win_k': nrm(ks[9], win, 1.0),
        'state_win_v': nrm(ks[10], win, 1.0),
        'page_table': page_table,
        'norm1_g': 1.0 + nrm(ks[11], (DEPTH, D_MODEL), 0.05),
        'w_in': nrm(ks[12], (DEPTH, D_MODEL, IN_W), D_MODEL ** -0.5),
        'cmp_pool_k': (1.0 + nrm(ks[13], pool_shape, 0.5)) * CMP_LEN ** -0.5,
        'cmp_pool_v': (1.0 + nrm(ks[14], pool_shape, 0.5)) * CMP_LEN ** -0.5,
        'cmp_proj_k': nrm(ks[15], (DEPTH, NSA_GROUPS, NSA_DIM, NSA_DIM), NSA_DIM ** -0.5),
        'cmp_proj_v': nrm(ks[16], (DEPTH, NSA_GROUPS, NSA_DIM, NSA_DIM), NSA_DIM ** -0.5),
        'w_up_sb': nrm(ks[17], (DEPTH, SB_W, D_MODEL), SB_W ** -0.5),
        'w_up_nsa': nrm(ks[18], (DEPTH, NSA_QW, D_MODEL), NSA_QW ** -0.5),
        'w_out': nrm(ks[19], (DEPTH, D_MODEL, D_MODEL), D_MODEL ** -0.5),
        'norm2_g': 1.0 + nrm(ks[20], (DEPTH, D_MODEL), 0.05),
        'w_ff_in': nrm(ks[21], (DEPTH, D_MODEL, D_FF), D_MODEL ** -0.5),
        'w_ff_out': nrm(ks[22], (DEPTH, D_FF, D_MODEL), D_FF ** -0.5),
        'norm_f_g': 1.0 + nrm(ks[23], (D_MODEL,), 0.05),
    }


def reference(x_prompt, x_sample, cache_sb_k, cache_sb_v, cache_cmp_k, cache_cmp_v,
              cache_slc_k, cache_slc_v, state_win_k, state_win_v, page_table,
              norm1_g, w_in, cmp_pool_k, cmp_pool_v, cmp_proj_k, cmp_proj_v,
              w_up_sb, w_up_nsa, w_out, norm2_g, w_ff_in, w_ff_out, norm_f_g):
    h_p, h_s = x_prompt, x_sample
    new_p, new_s = [], []
    caches = (cache_sb_k, cache_sb_v, cache_cmp_k, cache_cmp_v, cache_slc_k, cache_slc_v)
    for l in range(DEPTH):
        wl = (norm1_g[l], w_in[l], cmp_pool_k[l], cmp_pool_v[l], cmp_proj_k[l], cmp_proj_v[l],
              w_up_sb[l], w_up_nsa[l], w_out[l], norm2_g[l], w_ff_in[l], w_ff_out[l])
        h_p, st_p = prompt_layer(h_p, wl)
        h_s, st_s = sample_layer(h_s, l, caches, state_win_k[l], state_win_v[l], page_table, wl)
        new_p.append(st_p)
        new_s.append(st_s)
    stk = lambda lst, j: jnp.stack([st[j] for st in lst])
    y_prompt = rmsnorm(h_p, norm_f_g)
    y_sample = rmsnorm(h_s, norm_f_g)
    p_sb_k, p_sb_v, p_cmp_k, p_cmp_v = stk(new_p, 0), stk(new_p, 1), stk(new_p, 2), stk(new_p, 3)
    p_slc_k, p_slc_v, p_win_k, p_win_v = stk(new_p, 4), stk(new_p, 5), stk(new_p, 6), stk(new_p, 7)
    s_sb_k, s_sb_v, s_cmp_k, s_cmp_v = stk(new_s, 0), stk(new_s, 1), stk(new_s, 2), stk(new_s, 3)
    s_slc_k, s_slc_v, s_win_k, s_win_v = stk(new_s, 4), stk(new_s, 5), stk(new_s, 6), stk(new_s, 7)
    return (y_prompt, y_sample,
            p_sb_k, p_sb_v, p_cmp_k, p_cmp_v, p_slc_k, p_slc_v, p_win_k, p_win_v,
            s_sb_k, s_sb_v, s_cmp_k, s_cmp_v, s_slc_k, s_slc_v, s_win_k, s_win_v)
```

```python
import functools

import numpy as np
import jax
import jax.numpy as jnp
from jax import lax
from jax.experimental import pallas as pl
from jax.experimental.pallas import tpu as pltpu

SB_HEADS = 8
SB_DIM = 64
NSA_HEADS = 8
NSA_GROUPS = 2
NSA_DIM = 64
HEADS_PER_GROUP = NSA_HEADS // NSA_GROUPS
CMP_LEN = 32
CMP_STRIDE = 16
CMP_RATIO = CMP_LEN // CMP_STRIDE
SLC_BLOCK = 64
N_SEL = 16
WINDOW = 512
N_NSA_BRANCH = 3
EPS = 1e-6
NEG = -1e30
FORCE = 1e4
LOWEST = -3e38

LANES = 128
VMEM_LIMIT = 56 * 1024 * 1024

SB_W = SB_HEADS * SB_DIM
NSA_QW = NSA_HEADS * NSA_DIM
NSA_KVW = NSA_GROUPS * NSA_DIM

F32 = jnp.float32
BF16 = jnp.bfloat16
NT_DIMS = (((1,), (1,)), ((), ()))


def _dot(a, b):
    return jnp.dot(a, b, preferred_element_type=F32)


def _dot_nt(a, b):
    return lax.dot_general(a, b, NT_DIMS, preferred_element_type=F32)


def _split_bf16(x):
    hi = x.astype(BF16)
    lo = (x - hi.astype(F32)).astype(BF16)
    return hi, lo


def _params(*sem):
    return pltpu.CompilerParams(dimension_semantics=sem, vmem_limit_bytes=VMEM_LIMIT)


def _const_spec(shape):
    nd = len(shape)
    return pl.BlockSpec(shape, lambda *_: (0,) * nd, pipeline_mode=pl.Buffered(1))


def _channel_major(t):
    lead = t.shape[:-3]
    n = len(lead)
    t = jnp.transpose(t, tuple(range(n)) + (n + 1, n + 2, n))
    return t.reshape(lead + (t.shape[-3] * t.shape[-2], t.shape[-1]))


def _token_major(t, groups):
    lead = t.shape[:-2]
    n = len(lead)
    t = t.reshape(lead + (groups, t.shape[-2] // groups, t.shape[-1]))
    return jnp.transpose(t, tuple(range(n)) + (n + 2, n, n + 1))


_C_SQ = 0
_C_NQ = _C_SQ + SB_W
_C_CK = _C_NQ + NSA_HEADS * LANES
_C_CV = _C_CK + NSA_KVW
_C_LK = _C_CV + NSA_KVW
_C_WK = _C_LK + NSA_KVW
_C_MG = _C_WK + NSA_KVW
_R_SK = 0
_R_SV = _R_SK + SB_W
_R_CK = _R_SV + SB_W
_R_CV = _R_CK + NSA_KVW
_R_LK = _R_CV + NSA_KVW
_R_LV = _R_LK + NSA_KVW
_R_WK = _R_LV + NSA_KVW
_R_WV = _R_WK + NSA_KVW
_R_NG = _R_WV + NSA_KVW
NG_ROWS = 16
_R_END = _R_NG + NSA_GROUPS * NG_ROWS


def _proj_kernel(x_ref, g_ref, w_ref, wt_ref,
                 sq_o, nq_o, ck_o, cv_o, lkb_o, wkb_o, mg_o,
                 skt_o, sktb_o, svt_o, svtb_o, ckt_o, cvt_o, lkt_o, lvt_o, lvtb_o, wkt_o, wvt_o, wvtb_o, ngt_o,
                 *, d_mg):
    x = x_ref[0]
    ms = jnp.mean(x * x, axis=-1, keepdims=True)
    xn = (x * lax.rsqrt(ms + EPS) * g_ref[...]).astype(BF16)

    def cols(c0, width):
        return _dot(xn, w_ref[:, c0:c0 + width])

    def rows(r0, height):
        return _dot_nt(wt_ref[r0:r0 + height, :], xn)

    sq_o[0] = (cols(_C_SQ, SB_W) * (SB_DIM ** -0.5)).astype(BF16)
    nq_o[0] = (cols(_C_NQ, NSA_HEADS * LANES) * (NSA_DIM ** -0.5)).astype(BF16)
    ck_o[0] = cols(_C_CK, NSA_KVW)
    cv_o[0] = cols(_C_CV, NSA_KVW)
    lkb_o[0] = cols(_C_LK, NSA_KVW).astype(BF16)
    wkb_o[0] = cols(_C_WK, NSA_KVW).astype(BF16)
    mg_o[0] = cols(_C_MG, d_mg)
    skt = rows(_R_SK, SB_W)
    skt_o[0] = skt
    sktb_o[0] = skt.astype(BF16)
    svt = rows(_R_SV, SB_W)
    svt_o[0] = svt
    svtb_o[0] = svt.astype(BF16)
    ckt_o[0] = rows(_R_CK, NSA_KVW)
    cvt_o[0] = rows(_R_CV, NSA_KVW)
    lkt_o[0] = rows(_R_LK, NSA_KVW)
    lvt = rows(_R_LV, NSA_KVW)
    lvt_o[0] = lvt
    lvtb_o[0] = lvt.astype(BF16)
    wkt_o[0] = rows(_R_WK, NSA_KVW)
    wvt = rows(_R_WV, NSA_KVW)
    wvt_o[0] = wvt
    wvtb_o[0] = wvt.astype(BF16)
    ngt_o[0] = rows(_R_NG, NSA_GROUPS * NG_ROWS)


def _pack_proj_weights(w_in, d_model):
    splits = np.cumsum([SB_W, SB_W, SB_W, NSA_QW, NSA_KVW, NSA_KVW, NSA_KVW, NSA_KVW, NSA_KVW, NSA_KVW,
                        NSA_HEADS * N_NSA_BRANCH])
    sq, sk, sv, nq, ck, cv, lk, lv, wk, wv, ng, mg = jnp.split(w_in, splits.tolist(), axis=1)
    nq4 = nq.reshape(d_model, NSA_GROUPS, HEADS_PER_GROUP, NSA_DIM)
    nq_pad = jnp.zeros((d_model, NSA_GROUPS, HEADS_PER_GROUP, NSA_GROUPS, NSA_DIM), w_in.dtype)
    for g in range(NSA_GROUPS):
        nq_pad = nq_pad.at[:, g, :, g, :].set(nq4[:, g])
    nq_pad = nq_pad.reshape(d_model, NSA_HEADS * LANES)
    w_main = jnp.concatenate([sq, nq_pad, ck, cv, lk, wk, mg], axis=1).astype(BF16)
    ng3 = ng.reshape(d_model, NSA_GROUPS, HEADS_PER_GROUP * N_NSA_BRANCH)
    ng3 = jnp.pad(ng3, ((0, 0), (0, 0), (0, NG_ROWS - HEADS_PER_GROUP * N_NSA_BRANCH)))
    w_t = jnp.concatenate([sk, sv, ck, cv, lk, lv, wk, wv, ng3.reshape(d_model, NSA_GROUPS * NG_ROWS)],
                          axis=1).T.astype(BF16)
    return w_main, w_t


def _project(x, norm_g, w_main, w_t, tm):
    b, s, d = x.shape
    d_mg = w_main.shape[1] - _C_MG
    row = lambda width: pl.BlockSpec((1, tm, width), lambda bi, i: (bi, i, 0))
    col = lambda rows: pl.BlockSpec((1, rows, tm), lambda bi, i: (bi, 0, i))
    tok = lambda width, dt: jax.ShapeDtypeStruct((b, s, width), dt)
    chn = lambda rows, dt: jax.ShapeDtypeStruct((b, rows, s), dt)
    out_shape = (tok(SB_W, BF16), tok(NSA_HEADS * LANES, BF16), tok(NSA_KVW, F32), tok(NSA_KVW, F32),
                 tok(NSA_KVW, BF16), tok(NSA_KVW, BF16), tok(d_mg, F32),
                 chn(SB_W, F32), chn(SB_W, BF16), chn(SB_W, F32), chn(SB_W, BF16),
                 chn(NSA_KVW, F32), chn(NSA_KVW, F32), chn(NSA_KVW, F32), chn(NSA_KVW, F32), chn(NSA_KVW, BF16),
                 chn(NSA_KVW, F32), chn(NSA_KVW, F32), chn(NSA_KVW, BF16), chn(NSA_GROUPS * NG_ROWS, F32))
    out_specs = (row(SB_W), row(NSA_HEADS * LANES), row(NSA_KVW), row(NSA_KVW), row(NSA_KVW), row(NSA_KVW),
                 row(d_mg),
                 col(SB_W), col(SB_W), col(SB_W), col(SB_W),
                 col(NSA_KVW), col(NSA_KVW), col(NSA_KVW), col(NSA_KVW), col(NSA_KVW),
                 col(NSA_KVW), col(NSA_KVW), col(NSA_KVW), col(NSA_GROUPS * NG_ROWS))
    names = ("sq", "nq", "ck", "cv", "lkb", "wkb", "mg", "skt", "sktb", "svt", "svtb", "ckt", "cvt", "lkt", "lvt",
             "lvtb", "wkt", "wvt", "wvtb", "ngt")
    outs = pl.pallas_call(
        functools.partial(_proj_kernel, d_mg=d_mg),
        grid=(b, s // tm),
        in_specs=[row(d), _const_spec((1, d)), _const_spec(w_main.shape), _const_spec(w_t.shape)],
        out_specs=out_specs,
        out_shape=out_shape,
        compiler_params=_params("parallel", "parallel"),
        name="in_projection",
    )(x, norm_g.reshape(1, d), w_main, w_t)
    return dict(zip(names, outs))


def _compress_kernel(ck_ref, cv_ref, pk_ref, pv_ref, wk_ref, wvt_ref, kc_o, vct_o):
    n_sub = ck_ref.shape[1] // CMP_STRIDE

    def pooled(src, pw_ref):
        pw = pw_ref[...]
        a = jnp.zeros((n_sub, NSA_KVW), F32)
        b = jnp.zeros((n_sub, NSA_KVW), F32)
        for r in range(CMP_STRIDE):
            rows = src[0, pl.ds(r, n_sub, stride=CMP_STRIDE), :]
            a = a + rows * pw[r:r + 1, :]
            b = b + rows * pw[CMP_STRIDE + r:CMP_STRIDE + r + 1, :]
        return a + pltpu.roll(b, n_sub - 1, axis=0)

    kc_o[0] = _dot(pooled(ck_ref, pk_ref).astype(BF16), wk_ref[...]).astype(BF16)
    vct_o[0] = _dot_nt(wvt_ref[...], pooled(cv_ref, pv_ref).astype(BF16)).astype(BF16)


def _block_diag(proj):
    out = jnp.zeros((NSA_KVW, NSA_KVW), proj.dtype)
    for g in range(NSA_GROUPS):
        out = out.at[g * NSA_DIM:(g + 1) * NSA_DIM, g * NSA_DIM:(g + 1) * NSA_DIM].set(proj[g])
    return out


def _compress_prompt(ck, cv, pool_k, pool_v, wbd_k, wbd_v):
    b, s, _ = ck.shape
    n_sub = s // CMP_STRIDE
    kv = pl.BlockSpec((1, s, NSA_KVW), lambda bi: (bi, 0, 0))
    return pl.pallas_call(
        _compress_kernel,
        grid=(b,),
        in_specs=[kv, kv, _const_spec((CMP_LEN, NSA_KVW)), _const_spec((CMP_LEN, NSA_KVW)),
                  _const_spec((NSA_KVW, NSA_KVW)), _const_spec((NSA_KVW, NSA_KVW))],
        out_specs=(pl.BlockSpec((1, n_sub, NSA_KVW), lambda bi: (bi, 0, 0)),
                   pl.BlockSpec((1, NSA_KVW, n_sub), lambda bi: (bi, 0, 0))),
        out_shape=(jax.ShapeDtypeStruct((b, n_sub, NSA_KVW), BF16),
                   jax.ShapeDtypeStruct((b, NSA_KVW, n_sub), BF16)),
        compiler_params=_params("parallel"),
        name="compress_prompt",
    )(ck, cv, pool_k, pool_v, wbd_k.astype(BF16), wbd_v.T.astype(BF16))


def _softplus(z):
    return jnp.maximum(z, 0.0) + jnp.log(1.0 + jnp.exp(-jnp.abs(z)))


def _strict_lower(n):
    return jnp.asarray(np.tril(np.ones((n, n), np.float32), -1), BF16)


def _sb_prompt_kernel(q_ref, kt_ref, vt_ref, tri_ref, o_ref, *, t):
    i = pl.program_id(2)
    q = q_ref[0]
    lane = lax.broadcasted_iota(jnp.int32, (t, LANES), 1)
    zero = jnp.zeros_like(q)
    q_heads = (jnp.where(lane < SB_DIM, q, zero), jnp.where(lane >= SB_DIM, q, zero))
    tri = tri_ref[...]
    row = lax.broadcasted_iota(jnp.int32, (t, t), 0)
    colm = lax.broadcasted_iota(jnp.int32, (t, t), 1)

    def tile(j, carry, diag):
        off = pl.multiple_of(j * t, t)
        kt = kt_ref[0, :, pl.ds(off, t)]
        vt = vt_ref[0, :, pl.ds(off, t)]
        new = []
        for qx, (r, acc) in zip(q_heads, carry):
            z = _dot(qx, kt)
            sp = _softplus(z)
            log_keep = -sp
            if diag:
                causal = colm < row
                log_keep = jnp.where(causal, log_keep, 0.0)
            hi, lo = _split_bf16(log_keep)
            log_pass = _dot(hi, tri) + _dot(lo, tri) + r
            a = jnp.exp(z - sp + log_pass)
            if diag:
                a = jnp.where(causal, a, 0.0)
            acc = acc + _dot_nt(a.astype(BF16), vt)
            r = r + jnp.sum(log_keep, axis=-1, keepdims=True)
            new.append((r, acc))
        return tuple(new)

    init = tuple((jnp.zeros((t, 1), F32), jnp.zeros((t, LANES), F32)) for _ in q_heads)
    carry = tile(i, init, True)
    carry = lax.fori_loop(0, i, lambda jj, c: tile(i - 1 - jj, c, False), carry)
    o_ref[0] = jnp.where(lane < SB_DIM, carry[0][1], carry[1][1]).astype(o_ref.dtype)


def _sb_prompt(sq, skt, svt, t):
    b, s, w = sq.shape
    qo = pl.BlockSpec((1, t, LANES), lambda bi, hp, i: (bi, i, hp))
    kv = pl.BlockSpec((1, LANES, s), lambda bi, hp, i: (bi, hp, 0))
    return pl.pallas_call(
        functools.partial(_sb_prompt_kernel, t=t),
        grid=(b, w // LANES, s // t),
        in_specs=[qo, kv, kv, _const_spec((t, t))],
        out_specs=qo,
        out_shape=jax.ShapeDtypeStruct((b, s, w), BF16),
        compiler_params=_params("parallel", "parallel", "arbitrary"),
        name="sb_prompt",
    )(sq, skt, svt, _strict_lower(t))


def _sb_sample_kernel(pt_ref, q_ref, *refs, n_pc):
    k_refs = refs[:n_pc]
    v_refs = refs[n_pc:2 * n_pc]
    tri_ref = refs[2 * n_pc]
    o_ref = refs[2 * n_pc + 1]
    r_ref, acc_ref = refs[2 * n_pc + 2:]
    c = pl.program_id(1)

    @pl.when(c == 0)
    def _():
        r_ref[...] = jnp.zeros_like(r_ref)
        acc_ref[...] = jnp.zeros_like(acc_ref)

    head = lax.broadcasted_iota(jnp.int32, (SB_HEADS, SB_W), 0)
    lane_head = lax.broadcasted_iota(jnp.int32, (SB_HEADS, SB_W), 1) // SB_DIM
    own = head == lane_head
    qbd = jnp.where(own, jnp.broadcast_to(q_ref[0], (SB_HEADS, SB_W)), 0.0).astype(BF16)
    tri = tri_ref[...]
    r = r_ref[:, 0:1]
    acc = acc_ref[...]
    for ii in range(n_pc):
        z = _dot(qbd, k_refs[ii][0].astype(BF16))
        sp = _softplus(z)
        log_keep = -sp
        hi, lo = _split_bf16(log_keep)
        log_pass = _dot(hi, tri) + _dot(lo, tri) + r
        a = jnp.exp(z - sp + log_pass)
        acc = acc + _dot_nt(a.astype(BF16), v_refs[ii][0].astype(BF16))
        r = r + jnp.sum(log_keep, axis=-1, keepdims=True)
    r_ref[...] = jnp.broadcast_to(r, r_ref.shape)
    acc_ref[...] = acc

    @pl.when(c == pl.num_programs(1) - 1)
    def _():
        o_ref[0] = jnp.sum(jnp.where(own, acc, 0.0), axis=0, keepdims=True)


def _sb_sample(sq_s, cache_kt, cache_vt, pt_flat, layer, n_pool, n_pages, n_pc):
    db = sq_s.shape[0]
    page = cache_kt.shape[-1]
    n_chunks = n_pages // n_pc

    def page_spec(ii):
        def imap(bi, c, pt):
            p = n_pages - 1 - (c * n_pc + ii)
            return (layer * n_pool + jnp.clip(pt[bi * n_pages + p], 0, n_pool - 1), 0, 0)
        return pl.BlockSpec((1, SB_W, page), imap)

    specs = [pl.BlockSpec((1, 1, SB_W), lambda bi, c, pt: (bi, 0, 0))]
    specs += [page_spec(ii) for ii in range(n_pc)] * 2
    specs += [pl.BlockSpec((page, page), lambda bi, c, pt: (0, 0))]
    grid_spec = pltpu.PrefetchScalarGridSpec(
        num_scalar_prefetch=1, grid=(db, n_chunks), in_specs=specs,
        out_specs=pl.BlockSpec((1, 1, SB_W), lambda bi, c, pt: (bi, 0, 0)),
        scratch_shapes=[pltpu.VMEM((SB_HEADS, LANES), F32), pltpu.VMEM((SB_HEADS, SB_W), F32)])
    out = pl.pallas_call(
        functools.partial(_sb_sample_kernel, n_pc=n_pc),
        grid_spec=grid_spec,
        out_shape=jax.ShapeDtypeStruct((db, 1, SB_W), F32),
        compiler_params=_params("parallel", "arbitrary"),
        name="sb_sample",
    )(pt_flat, sq_s.reshape(db, 1, SB_W).astype(F32), *([cache_kt] * n_pc), *([cache_vt] * n_pc),
      _strict_lower(page))
    return out.reshape(db, SB_W)


def _rank_desc(v):
    n = v.shape[0]
    sub = lax.broadcasted_iota(jnp.int32, (8, v.shape[1]), 0)
    cnt = jnp.zeros(v.shape, jnp.int32)
    for m in range(n):
        rowv = v[m:m + 1, :]
        g0 = (m // 8) * 8
        parts = []
        if g0 > 0:
            parts.append(jnp.where(rowv > v[:g0], 1, 0))
        blk = v[g0:g0 + 8]
        parts.append(jnp.where(sub > (m % 8), jnp.where(rowv >= blk, 1, 0), jnp.where(rowv > blk, 1, 0)))
        if g0 + 8 < n:
            parts.append(jnp.where(rowv >= v[g0 + 8:], 1, 0))
        cnt = cnt + (parts[0] if len(parts) == 1 else jnp.concatenate(parts, axis=0))
    return cnt


def _cmp_to_slc_t(n_cmp, n_slc, n_cmp_pad, n_slc_pad):
    c_start = np.arange(n_cmp) * CMP_STRIDE
    s_start = np.arange(n_slc) * SLC_BLOCK
    inter = (np.minimum(c_start[None, :] + CMP_LEN, s_start[:, None] + SLC_BLOCK)
             - np.maximum(c_start[None, :], s_start[:, None]))
    ov = np.zeros((n_slc_pad, n_cmp_pad), np.float32)
    ov[:n_slc, :n_cmp] = np.clip(inter, 0, None) / CMP_LEN
    return ov


def _alibi_slopes():
    return (2.0 ** (-8.0 * (np.arange(NSA_HEADS) + 1) / NSA_HEADS)).astype(np.float32)


def _nsa_prompt_kernel(q_ref, lk_ref, lvt_ref, wk_ref, wvt_ref, kc_ref, vct_ref, ngt_ref, slope_ref, ovt_ref,
                       o_ref, sel_ref, *, tq, tks, tkw, n_sel):
    i = pl.program_id(2)
    nh = HEADS_PER_GROUP
    w = nh * tq
    q0 = i * tq
    qs = jnp.concatenate([q_ref[0, :, h * LANES:(h + 1) * LANES] for h in range(nh)], axis=0)
    lane = lax.broadcasted_iota(jnp.int32, (1, w), 1)
    qpos = q0 + (lane & (tq - 1))
    slope = slope_ref[0]

    kc = kc_ref[0]
    n_c = kc.shape[0]
    c_end = lax.broadcasted_iota(jnp.int32, (n_c, 1), 0) * CMP_STRIDE + (CMP_LEN - 1)
    dist = qpos - c_end
    valid = dist >= 0
    s = _dot_nt(kc, qs) - slope * dist.astype(F32)
    sm = jnp.where(valid, s, NEG)
    m = jnp.max(sm, axis=0, keepdims=True)
    e = jnp.where(valid, jnp.exp(sm - m), 0.0)
    l = jnp.sum(e, axis=0, keepdims=True)
    p = e * (1.0 / jnp.where(l > 0.0, l, 1.0))
    o_cmp = _dot(vct_ref[0], p.astype(BF16))

    p_sum = p[:, 0:tq]
    for h in range(1, nh):
        p_sum = p_sum + p[:, h * tq:(h + 1) * tq]
    hi, lo = _split_bf16(p_sum)
    ovt = ovt_ref[...]
    imp = _dot(ovt, hi) + _dot(ovt, lo)
    n_slc = imp.shape[0]
    blk = lax.broadcasted_iota(jnp.int32, (n_slc, tq), 0)
    cur = (q0 + lax.broadcasted_iota(jnp.int32, (1, tq), 1)) // SLC_BLOCK
    forced = (blk == 0) | (blk == cur) | (blk == cur - 1)
    imp = jnp.where(forced, FORCE, jnp.where(blk <= cur, imp, -FORCE))
    sel_ref[...] = jnp.where(_rank_desc(imp) < n_sel, 1.0, 0.0)

    def attend(k, vt, valid_t, dist_t, carry):
        m_old, l_old, acc = carry
        s_t = _dot_nt(k, qs) - slope * dist_t.astype(F32)
        sm_t = jnp.where(valid_t, s_t, NEG)
        m_new = jnp.maximum(m_old, jnp.max(sm_t, axis=0, keepdims=True))
        alpha = jnp.exp(m_old - m_new)
        p_t = jnp.exp(sm_t - m_new)
        l_new = alpha * l_old + jnp.sum(p_t, axis=0, keepdims=True)
        acc = alpha * acc + _dot(vt, p_t.astype(BF16))
        return m_new, l_new, acc

    init = (jnp.full((1, w), NEG, F32), jnp.zeros((1, w), F32), jnp.zeros((NSA_DIM, w), F32))

    def sel_tile(j, carry):
        off = pl.multiple_of(j * tks, tks)
        t_pos = off + lax.broadcasted_iota(jnp.int32, (tks, 1), 0)
        dist_t = qpos - t_pos
        rows = []
        for bb in range(tks // SLC_BLOCK):
            r = sel_ref[pl.ds(j * (tks // SLC_BLOCK) + bb, 1), :]
            rows.append(jnp.broadcast_to(r, (SLC_BLOCK, tq)))
        picked = jnp.concatenate(rows, axis=0)
        picked = jnp.concatenate([picked] * nh, axis=1)
        valid_t = jnp.where(picked > 0.5, dist_t, -1) >= 0
        return attend(lk_ref[0, pl.ds(off, tks), :], lvt_ref[0, :, pl.ds(off, tks)], valid_t, dist_t, carry)

    n_sel_tiles = (q0 + tq - 1) // tks + 1
    _, l_sel, acc_sel = lax.fori_loop(0, n_sel_tiles, sel_tile, init)

    def win_tile(d, carry):
        off = pl.multiple_of((i - d) * tkw, tkw)
        t_pos = off + lax.broadcasted_iota(jnp.int32, (tkw, 1), 0)
        dist_t = qpos - t_pos
        valid_t = jnp.where(dist_t <= WINDOW, dist_t, -1) >= 0
        return attend(wk_ref[0, pl.ds(off, tkw), :], wvt_ref[0, :, pl.ds(off, tkw)], valid_t, dist_t, carry)

    n_win_tiles = jnp.minimum(i, WINDOW // tkw) + 1
    _, l_win, acc_win = lax.fori_loop(0, n_win_tiles, win_tile, init)

    gates = jax.nn.sigmoid(ngt_ref[0])
    o_sel = acc_sel * (1.0 / l_sel)
    o_win = acc_win * (1.0 / l_win)
    heads = []
    for h in range(nh):
        hs = slice(h * tq, (h + 1) * tq)
        g0 = gates[h * N_NSA_BRANCH + 0:h * N_NSA_BRANCH + 1, :]
        g1 = gates[h * N_NSA_BRANCH + 1:h * N_NSA_BRANCH + 2, :]
        g2 = gates[h * N_NSA_BRANCH + 2:h * N_NSA_BRANCH + 3, :]
        heads.append(g0 * o_cmp[:, hs] + g1 * o_sel[:, hs] + g2 * o_win[:, hs])
    for hp in range(nh // 2):
        pair = jnp.concatenate([heads[2 * hp], heads[2 * hp + 1]], axis=0)
        o_ref[0, :, hp * LANES:(hp + 1) * LANES] = pair.T.astype(o_ref.dtype)


def _nsa_prompt(nq_pad, lk_b, lvt, wk_b, wvt, kc, vct, ngt, tq, tks):
    b, s, _ = nq_pad.shape
    assert tq == LANES and s % tks == 0 and tks % SLC_BLOCK == 0
    n_c = kc.shape[1]
    n_slc = s // SLC_BLOCK
    n_cmp = n_c - CMP_RATIO + 1
    w = HEADS_PER_GROUP * tq
    slopes = np.repeat(_alibi_slopes().reshape(NSA_GROUPS, HEADS_PER_GROUP), tq, axis=1).reshape(NSA_GROUPS, 1, w)
    ovt = jnp.asarray(_cmp_to_slc_t(n_cmp, n_slc, n_c, n_slc), BF16)
    full = lambda rows, lanes: pl.BlockSpec((1, rows, lanes), lambda bi, g, i: (bi, 0, 0))
    grp = lambda rows, lanes: pl.BlockSpec((1, rows, lanes), lambda bi, g, i: (bi, g, 0))
    return pl.pallas_call(
        functools.partial(_nsa_prompt_kernel, tq=tq, tks=tks, tkw=tq, n_sel=min(N_SEL, n_slc)),
        grid=(b, NSA_GROUPS, s // tq),
        in_specs=[pl.BlockSpec((1, tq, HEADS_PER_GROUP * LANES), lambda bi, g, i: (bi, i, g)),
                  full(s, NSA_KVW), grp(NSA_DIM, s), full(s, NSA_KVW), grp(NSA_DIM, s),
                  full(n_c, NSA_KVW), grp(NSA_DIM, n_c),
                  pl.BlockSpec((1, NG_ROWS, tq), lambda bi, g, i: (bi, g, i)),
                  pl.BlockSpec((1, 1, w), lambda bi, g, i: (g, 0, 0)),
                  _const_spec((n_slc, n_c))],
        out_specs=pl.BlockSpec((1, tq, HEADS_PER_GROUP * NSA_DIM), lambda bi, g, i: (bi, i, g)),
        out_shape=jax.ShapeDtypeStruct((b, s, NSA_QW), BF16),
        scratch_shapes=[pltpu.VMEM((n_slc, tq), F32)],
        compiler_params=_params("parallel", "parallel", "arbitrary"),
        name="nsa_prompt",
    )(nq_pad, lk_b, lvt, wk_b, wvt, kc, vct, ngt, jnp.asarray(slopes), ovt)


def _nsa_cmp_sample_kernel(pt_ref, q_ref, *refs, n_pc, past):
    k_refs = refs[:n_pc]
    v_refs = refs[n_pc:2 * n_pc]
    pk0_ref, pk1_ref, pv0_ref, pv1_ref, seg_ref, wkt_ref, wvt_ref, slope_ref, ov_ref = refs[2 * n_pc:2 * n_pc + 9]
    oc_o, imp_o = refs[2 * n_pc + 9:2 * n_pc + 11]
    ak_ref, bk_ref, av_ref, bv_ref = refs[2 * n_pc + 11:]
    c = pl.program_id(1)
    page = k_refs[0].shape[-1]
    m_chunk = n_pc * (page // CMP_STRIDE)

    def pool_chunk(page_refs, w0_ref, w1_ref):
        w0 = w0_ref[...]
        w1 = w1_ref[...]
        a = jnp.zeros((NSA_KVW, m_chunk), F32)
        b = jnp.zeros((NSA_KVW, m_chunk), F32)
        for pair in range(n_pc // 2):
            xa = page_refs[2 * pair][0]
            xb = page_refs[2 * pair + 1][0]
            seg = seg_ref[pair]
            a = a + _dot(jnp.concatenate([xa * w0, xb * w0], axis=1).astype(BF16), seg)
            b = b + _dot(jnp.concatenate([xa * w1, xb * w1], axis=1).astype(BF16), seg)
        return a, b

    off = pl.multiple_of(c * m_chunk, m_chunk)
    a, b = pool_chunk(k_refs, pk0_ref, pk1_ref)
    ak_ref[:, pl.ds(off, m_chunk)] = a
    bk_ref[:, pl.ds(off, m_chunk)] = b
    a, b = pool_chunk(v_refs, pv0_ref, pv1_ref)
    av_ref[:, pl.ds(off, m_chunk)] = a
    bv_ref[:, pl.ds(off, m_chunk)] = b

    @pl.when(c == pl.num_programs(1) - 1)
    def _():
        n_sub = ak_ref.shape[1]
        pooled_k = ak_ref[...] + pltpu.roll(bk_ref[...], n_sub - 1, axis=1)
        pooled_v = av_ref[...] + pltpu.roll(bv_ref[...], n_sub - 1, axis=1)
        kct = _dot(wkt_ref[...], pooled_k.astype(BF16)).astype(BF16)
        vct = _dot(wvt_ref[...], pooled_v.astype(BF16)).astype(BF16)
        qn = q_ref[0].astype(BF16)
        c_end = lax.broadcasted_iota(jnp.int32, (1, n_sub), 1) * CMP_STRIDE + (CMP_LEN - 1)
        dist = past - c_end
        valid = dist >= 0
        s = _dot(qn, kct) - slope_ref[:, 0:1] * dist.astype(F32)
        sm = jnp.where(valid, s, NEG)
        m = jnp.max(sm, axis=-1, keepdims=True)
        e = jnp.where(valid, jnp.exp(sm - m), 0.0)
        l = jnp.sum(e, axis=-1, keepdims=True)
        p = e * (1.0 / jnp.where(l > 0.0, l, 1.0))
        oc_o[0] = _dot_nt(p.astype(BF16), vct)
        sums = [jnp.sum(p[g * HEADS_PER_GROUP:(g + 1) * HEADS_PER_GROUP], axis=0, keepdims=True)
                for g in range(NSA_GROUPS)]
        sums.append(jnp.zeros((NSA_HEADS - NSA_GROUPS, n_sub), F32))
        hi, lo = _split_bf16(jnp.concatenate(sums, axis=0))
        ov = ov_ref[...]
        imp_o[0] = _dot(hi, ov) + _dot(lo, ov)


def _nsa_cmp_sample(qn_s, cache_kt, cache_vt, pt_flat, layer, n_pool, n_pages, n_pc, pool_k, pool_v, wbd_k, wbd_v,
                    n_slc_pad):
    db = qn_s.shape[0]
    page = cache_kt.shape[-1]
    past = n_pages * page
    n_sub = past // CMP_STRIDE
    n_cmp = n_sub - CMP_RATIO + 1
    n_slc = -(-(past + 1) // SLC_BLOCK)
    sub_per_page = page // CMP_STRIDE
    m_chunk = n_pc * sub_per_page
    assert n_pc % 2 == 0 and m_chunk % LANES == 0
    ov = jnp.asarray(_cmp_to_slc_t(n_cmp, n_slc, n_sub, n_slc_pad).T, BF16)
    slopes = jnp.asarray(np.repeat(_alibi_slopes()[:, None], LANES, axis=1))
    t_idx = np.arange(2 * page)
    seg = np.zeros((n_pc // 2, 2 * page, m_chunk), np.float32)
    for pair in range(n_pc // 2):
        seg[pair, t_idx, 2 * sub_per_page * pair + t_idx // CMP_STRIDE] = 1.0
    tile_w = lambda pw, half: jnp.tile(pw[half * CMP_STRIDE:(half + 1) * CMP_STRIDE].T, (1, sub_per_page))

    def page_spec(ii):
        return pl.BlockSpec((1, NSA_KVW, page),
                            lambda bi, c, pt: (layer * n_pool
                                               + jnp.clip(pt[bi * n_pages + c * n_pc + ii], 0, n_pool - 1), 0, 0))

    cst = lambda shape: pl.BlockSpec(shape, lambda bi, c, pt: (0,) * len(shape))
    specs = [pl.BlockSpec((1, NSA_HEADS, LANES), lambda bi, c, pt: (bi, 0, 0))]
    specs += [page_spec(ii) for ii in range(n_pc)] * 2
    specs += [cst((NSA_KVW, page))] * 4
    specs += [cst(seg.shape), cst((NSA_KVW, NSA_KVW)), cst((NSA_KVW, NSA_KVW)),
              cst((NSA_HEADS, LANES)), cst((n_sub, n_slc_pad))]
    out_blk = lambda lanes: pl.BlockSpec((1, NSA_HEADS, lanes), lambda bi, c, pt: (bi, 0, 0))
    grid_spec = pltpu.PrefetchScalarGridSpec(
        num_scalar_prefetch=1, grid=(db, n_pages // n_pc), in_specs=specs,
        out_specs=(out_blk(LANES), out_blk(n_slc_pad)),
        scratch_shapes=[pltpu.VMEM((NSA_KVW, n_sub), F32)] * 4)
    return pl.pallas_call(
        functools.partial(_nsa_cmp_sample_kernel, n_pc=n_pc, past=past),
        grid_spec=grid_spec,
        out_shape=(jax.ShapeDtypeStruct((db, NSA_HEADS, LANES), F32),
                   jax.ShapeDtypeStruct((db, NSA_HEADS, n_slc_pad), F32)),
        compiler_params=_params("parallel", "arbitrary"),
        name="nsa_cmp_sample",
    )(pt_flat, qn_s, *([cache_kt] * n_pc), *([cache_vt] * n_pc),
      tile_w(pool_k, 0), tile_w(pool_k, 1), tile_w(pool_v, 0), tile_w(pool_v, 1), jnp.asarray(seg, BF16),
      wbd_k.T.astype(BF16), wbd_v.T.astype(BF16), slopes, ov)


def _topk_sample_kernel(imp_ref, idx_o, *, n_slc, cur, n_sel):
    imp = imp_ref[...]
    blk = lax.broadcasted_iota(jnp.int32, imp.shape, 0)
    forced = (blk == 0) | (blk == cur) | (blk == cur - 1)
    imp = jnp.where(forced, FORCE, jnp.where(blk <= cur, imp, -FORCE))
    imp = jnp.where(blk < n_slc, imp, LOWEST)
    rank = _rank_desc(imp)
    rows = [jnp.sum(jnp.where(rank == r, blk, 0), axis=0, keepdims=True) for r in range(n_sel)]
    idx_o[...] = jnp.concatenate(rows, axis=0)


def _topk_sample(imp_t, n_slc, cur, n_sel):
    n_pad, cols = imp_t.shape
    return pl.pallas_call(
        functools.partial(_topk_sample_kernel, n_slc=n_slc, cur=cur, n_sel=n_sel),
        grid=(1,),
        in_specs=[_const_spec((n_pad, cols))],
        out_specs=_const_spec((n_sel, cols)),
        out_shape=jax.ShapeDtypeStruct((n_sel, cols), jnp.int32),
        compiler_params=_params("arbitrary"),
        name="topk_sample",
    )(imp_t)


def _nsa_sel_sample_kernel(pt_ref, idx_ref, q_ref, *refs, n_sel, n_cached, past, n_buf):
    n_blk = NSA_GROUPS * n_sel
    k_refs = refs[:n_blk]
    v_refs = refs[n_blk:2 * n_blk]
    wink_ref, winv_ref, new_ref, ng_ref, oc_ref, slope_ref, o_ref = refs[2 * n_blk:]
    bi = pl.program_id(0)
    page = k_refs[0].shape[-1]
    halves = page // SLC_BLOCK
    qn = q_ref[0]
    qb = qn.astype(BF16)
    rowi = lax.broadcasted_iota(jnp.int32, (NSA_HEADS, 1), 0)
    in_g0 = rowi < HEADS_PER_GROUP
    slope = slope_ref[:, 0:1]
    lk_new, lv_new, wk_new, wv_new = (new_ref[0, r:r + 1, :] for r in range(4))

    def new_score(k_row):
        return jnp.sum(qn * k_row, axis=-1, keepdims=True)

    width = n_sel * page
    lane = lax.broadcasted_iota(jnp.int32, (1, width), 1)
    slot = lane // page
    scores, values, blk_rows, has_new = [], [], [], []
    for g in range(NSA_GROUPS):
        ids = [idx_ref[(bi * NSA_GROUPS + g) * n_sel + r] for r in range(n_sel)]
        scores.append(jnp.concatenate(
            [_dot(qb, k_refs[g * n_sel + r][0].astype(BF16)) for r in range(n_sel)], axis=1))
        values.append(jnp.concatenate([v_refs[g * n_sel + r][0] for r in range(n_sel)], axis=1).astype(BF16))
        row = jnp.zeros((1, width), jnp.int32)
        flag = jnp.int32(0)
        for r in range(n_sel):
            row = jnp.where(slot == r, ids[r], row)
            flag = jnp.maximum(flag, jnp.where(ids[r] >= n_cached, 1, 0))
        blk_rows.append(row)
        has_new.append(flag)
    s = jnp.where(in_g0, scores[0], scores[1])
    blk = jnp.where(in_g0, blk_rows[0], blk_rows[1])
    page_blk = jnp.minimum(blk, n_cached - 1)
    tok = (page_blk // halves) * page + (lane & (page - 1))
    dist = past - tok
    valid = jnp.where(blk < n_cached, tok // SLC_BLOCK, -1) == blk
    s = s - slope * dist.astype(F32)
    sm = jnp.where(valid, s, NEG)
    new_on = jnp.where(in_g0, has_new[0], has_new[1]) > 0
    s_new = jnp.where(new_on, new_score(lk_new), NEG)
    m = jnp.maximum(jnp.max(sm, axis=-1, keepdims=True), s_new)
    e = jnp.where(valid, jnp.exp(sm - m), 0.0)
    e_new = jnp.where(new_on, jnp.exp(s_new - m), 0.0)
    l = jnp.sum(e, axis=-1, keepdims=True) + e_new
    eb = e.astype(BF16)
    o_sel = jnp.where(in_g0, _dot_nt(eb, values[0]), _dot_nt(eb, values[1])) + e_new * lv_new
    o_sel = o_sel * (1.0 / l)

    pos = lax.broadcasted_iota(jnp.int32, (1, n_buf), 1)
    dist_w = n_buf - pos
    valid_w = jnp.where(dist_w <= WINDOW, past - dist_w, -1) >= 0
    s_w = _dot(qb, wink_ref[0].astype(BF16)) - slope * dist_w.astype(F32)
    sm_w = jnp.where(valid_w, s_w, NEG)
    s_wn = new_score(wk_new)
    m_w = jnp.maximum(jnp.max(sm_w, axis=-1, keepdims=True), s_wn)
    e_w = jnp.where(valid_w, jnp.exp(sm_w - m_w), 0.0)
    e_wn = jnp.exp(s_wn - m_w)
    l_w = jnp.sum(e_w, axis=-1, keepdims=True) + e_wn
    o_win = (_dot_nt(e_w.astype(BF16), winv_ref[0].astype(BF16)) + e_wn * wv_new) * (1.0 / l_w)

    gates = jax.nn.sigmoid(ng_ref[0])
    o = gates[:, 0:1] * oc_ref[0] + gates[:, 1:2] * o_sel + gates[:, 2:3] * o_win
    lane_g = lax.broadcasted_iota(jnp.int32, (NSA_HEADS, LANES), 1) // NSA_DIM
    o_ref[0] = jnp.where(lane_g == rowi // HEADS_PER_GROUP, o, 0.0)


def _nsa_sel_sample(qn_s, cache_kt, cache_vt, win_kt, win_vt, new_rows, ng_s, o_cmp, pt_flat, idx_flat, layer,
                    n_pool, n_pages):
    db = qn_s.shape[0]
    page = cache_kt.shape[-1]
    past = n_pages * page
    halves = page // SLC_BLOCK
    n_cached = past // SLC_BLOCK
    n_sel = idx_flat.shape[0] // (db * NSA_GROUPS)
    n_buf = win_kt.shape[-1]
    slopes = jnp.asarray(np.repeat(_alibi_slopes()[:, None], LANES, axis=1))

    def blk_spec(g, r):
        def imap(bi, pt, idx):
            n = jnp.clip(idx[(bi * NSA_GROUPS + g) * n_sel + r], 0, n_cached - 1)
            return (layer * n_pool + jnp.clip(pt[bi * n_pages + n // halves], 0, n_pool - 1), 0, 0)
        return pl.BlockSpec((1, NSA_KVW, page), imap)

    per_req = lambda rows, lanes: pl.BlockSpec((1, rows, lanes), lambda bi, pt, idx: (bi, 0, 0))
    blocks = [blk_spec(g, r) for g in range(NSA_GROUPS) for r in range(n_sel)]
    specs = [per_req(NSA_HEADS, LANES)] + blocks + blocks
    specs += [per_req(NSA_KVW, n_buf), per_req(NSA_KVW, n_buf), per_req(8, NSA_KVW),
              per_req(NSA_HEADS, LANES), per_req(NSA_HEADS, LANES),
              pl.BlockSpec((NSA_HEADS, LANES), lambda bi, pt, idx: (0, 0))]
    grid_spec = pltpu.PrefetchScalarGridSpec(
        num_scalar_prefetch=2, grid=(db,), in_specs=specs, out_specs=per_req(NSA_HEADS, LANES))
    n_blk = len(blocks)
    return pl.pallas_call(
        functools.partial(_nsa_sel_sample_kernel, n_sel=n_sel, n_cached=n_cached, past=past, n_buf=n_buf),
        grid_spec=grid_spec,
        out_shape=jax.ShapeDtypeStruct((db, NSA_HEADS, LANES), F32),
        compiler_params=_params("parallel"),
        name="nsa_sel_sample",
    )(pt_flat, idx_flat, qn_s, *([cache_kt] * n_blk), *([cache_vt] * n_blk), win_kt, win_vt, new_rows, ng_s, o_cmp,
      slopes)


def _merge_kernel(x_ref, osb_ref, onsa_ref, mg_ref, wsb_ref, wnsa_ref, wout_ref, g2_ref, wfi_ref, wfo_ref, gf_ref,
                  y_ref, *, final_norm):
    d = x_ref.shape[-1]
    gate = jax.nn.sigmoid(mg_ref[0])
    mixed = gate[:, 0:d] * _dot(osb_ref[0], wsb_ref[...]) + gate[:, d:2 * d] * _dot(onsa_ref[0], wnsa_ref[...])
    x1 = x_ref[0] + _dot(mixed.astype(BF16), wout_ref[...])
    ms = jnp.mean(x1 * x1, axis=-1, keepdims=True)
    hn = (x1 * lax.rsqrt(ms + EPS) * g2_ref[...]).astype(BF16)
    u = jnp.maximum(_dot(hn, wfi_ref[...]), 0.0)
    x2 = x1 + _dot((u * u).astype(BF16), wfo_ref[...])
    if final_norm:
        ms2 = jnp.mean(x2 * x2, axis=-1, keepdims=True)
        x2 = x2 * lax.rsqrt(ms2 + EPS) * gf_ref[...]
    y_ref[0] = x2


def _merge(x, o_sb, o_nsa, mg, w_up_sb, w_up_nsa, w_out, norm2_g, w_ff_in, w_ff_out, norm_f_g, final_norm, tm):
    b, s, d = x.shape
    d_ff = w_ff_in.shape[1]
    row = lambda width: pl.BlockSpec((1, tm, width), lambda bi, i: (bi, i, 0))
    return pl.pallas_call(
        functools.partial(_merge_kernel, final_norm=final_norm),
        grid=(b, s // tm),
        in_specs=[row(d), row(SB_W), row(NSA_QW), row(2 * d),
                  _const_spec((SB_W, d)), _const_spec((NSA_QW, d)), _const_spec((d, d)), _const_spec((1, d)),
                  _const_spec((d, d_ff)), _const_spec((d_ff, d)), _const_spec((1, d))],
        out_specs=row(d),
        out_shape=jax.ShapeDtypeStruct((b, s, d), F32),
        compiler_params=_params("parallel", "parallel"),
        name="merge_mlp",
    )(x, o_sb, o_nsa, mg, w_up_sb, w_up_nsa, w_out, norm2_g.reshape(1, d), w_ff_in, w_ff_out, norm_f_g.reshape(1, d))


def _row_tile(n, target):
    t = min(n, target)
    assert n % t == 0
    return t


def _prompt_layer(x, wl, norm_f_g, final_norm):
    b, s, d = x.shape
    pr = _project(x, wl["norm1_g"], wl["w_main"], wl["w_t"], _row_tile(s, 256))
    kc, vct = _compress_prompt(pr["ck"], pr["cv"], wl["pool_k"], wl["pool_v"], wl["wbd_k"], wl["wbd_v"])
    o_sb = _sb_prompt(pr["sq"], pr["sktb"], pr["svtb"], _row_tile(s, 256))
    o_nsa = _nsa_prompt(pr["nq"], pr["lkb"], pr["lvtb"], pr["wkb"], pr["wvtb"], kc, vct, pr["ngt"], LANES, 256)
    y = _merge(x, o_sb, o_nsa, pr["mg"], wl["w_up_sb"], wl["w_up_nsa"], wl["w_out"], wl["norm2_g"],
               wl["w_ff_in"], wl["w_ff_out"], norm_f_g, final_norm, _row_tile(s, 256))
    n_keep = min(WINDOW, s)
    state = (_token_major(pr["skt"], SB_HEADS), _token_major(pr["svt"], SB_HEADS),
             _token_major(pr["ckt"], NSA_GROUPS), _token_major(pr["cvt"], NSA_GROUPS),
             _token_major(pr["lkt"], NSA_GROUPS), _token_major(pr["lvt"], NSA_GROUPS),
             _token_major(pr["wkt"][:, :, s - n_keep:], NSA_GROUPS),
             _token_major(pr["wvt"][:, :, s - n_keep:], NSA_GROUPS))
    return y, state


def _sample_layer(x, layer, caches_t, n_pool, win_k, win_v, page_table, wl, norm_f_g, final_norm):
    c_sb_k, c_sb_v, c_cmp_k, c_cmp_v, c_slc_k, c_slc_v = caches_t
    db, n_new, d = x.shape
    assert n_new == 1
    n_pages = page_table.shape[1]
    page = c_sb_k.shape[-1]
    past = n_pages * page
    pt_flat = page_table.reshape(-1).astype(jnp.int32)
    pr = _project(x.reshape(1, db, d), wl["norm1_g"], wl["w_main"], wl["w_t"], db)
    n_pc = 16 if n_pages % 16 == 0 else 2
    o_sb = _sb_sample(pr["sq"][0], c_sb_k, c_sb_v, pt_flat, layer, n_pool, n_pages, n_pc // 2)

    qn_s = pr["nq"][0].astype(F32).reshape(db, NSA_HEADS, LANES)
    n_slc = -(-(past + n_new) // SLC_BLOCK)
    cur = past // SLC_BLOCK
    n_slc_pad = -(-n_slc // LANES) * LANES
    o_cmp, imp = _nsa_cmp_sample(qn_s, c_cmp_k, c_cmp_v, pt_flat, layer, n_pool, n_pages, n_pc,
                                 wl["pool_k"], wl["pool_v"], wl["wbd_k"], wl["wbd_v"], n_slc_pad)
    n_rank = -(-n_slc // 8) * 8
    imp_t = imp[:, :NSA_GROUPS, :n_rank].reshape(db * NSA_GROUPS, n_rank).T
    n_sel = min(N_SEL, n_slc)
    idx = _topk_sample(imp_t, n_slc, cur, n_sel)
    idx_flat = idx.T.reshape(-1)

    new_t = lambda name: pr[name][0].T
    zeros = jnp.zeros((db, NSA_KVW), F32)
    new_rows = jnp.stack([new_t("lkt"), new_t("lvt"), new_t("wkt"), new_t("wvt"), zeros, zeros, zeros, zeros], axis=1)
    ng = pr["ngt"][0].reshape(NSA_GROUPS, NG_ROWS, db)[:, :HEADS_PER_GROUP * N_NSA_BRANCH]
    ng = ng.reshape(NSA_HEADS, N_NSA_BRANCH, db).transpose(2, 0, 1)
    ng_s = jnp.pad(ng, ((0, 0), (0, 0), (0, LANES - N_NSA_BRANCH)))
    win_kt = _channel_major(win_k)
    win_vt = _channel_major(win_v)
    o_all = _nsa_sel_sample(qn_s, c_slc_k, c_slc_v, win_kt, win_vt, new_rows, ng_s, o_cmp, pt_flat, idx_flat, layer,
                            n_pool, n_pages)
    o5 = o_all.reshape(db, NSA_GROUPS, HEADS_PER_GROUP, NSA_GROUPS, NSA_DIM)
    o_nsa = jnp.stack([o5[:, g, :, g, :] for g in range(NSA_GROUPS)], axis=1).reshape(1, db, NSA_QW)

    y = _merge(x.reshape(1, db, d), o_sb.reshape(1, db, SB_W).astype(BF16), o_nsa.astype(BF16), pr["mg"],
               wl["w_up_sb"], wl["w_up_nsa"], wl["w_out"], wl["norm2_g"], wl["w_ff_in"], wl["w_ff_out"],
               norm_f_g, final_norm, db)
    n_buf = win_k.shape[1]
    n_keep = min(WINDOW, past + n_new)
    new_state = lambda name, groups: _token_major(pr[name][0][None], groups)[0][:, None]
    wk_all = jnp.concatenate([win_k, new_state("wkt", NSA_GROUPS)], axis=1)
    wv_all = jnp.concatenate([win_v, new_state("wvt", NSA_GROUPS)], axis=1)
    state = (new_state("skt", SB_HEADS), new_state("svt", SB_HEADS),
             new_state("ckt", NSA_GROUPS), new_state("cvt", NSA_GROUPS),
             new_state("lkt", NSA_GROUPS), new_state("lvt", NSA_GROUPS),
             wk_all[:, n_buf + n_new - n_keep:], wv_all[:, n_buf + n_new - n_keep:])
    return y.reshape(db, n_new, d), state


def kernel(x_prompt, x_sample, cache_sb_k, cache_sb_v, cache_cmp_k, cache_cmp_v, cache_slc_k, cache_slc_v,
           state_win_k, state_win_v, page_table, norm1_g, w_in, cmp_pool_k, cmp_pool_v, cmp_proj_k, cmp_proj_v,
           w_up_sb, w_up_nsa, w_out, norm2_g, w_ff_in, w_ff_out, norm_f_g):
    depth = w_in.shape[0]
    d_model = x_prompt.shape[-1]
    n_pool = cache_sb_k.shape[1]
    h_p, h_s = x_prompt, x_sample
    pages = lambda c: _channel_major(c).reshape((depth * n_pool,) + (c.shape[3] * c.shape[4], c.shape[2]))
    caches_t = tuple(pages(c) for c in (cache_sb_k, cache_sb_v, cache_cmp_k, cache_cmp_v, cache_slc_k, cache_slc_v))
    new_p, new_s = [], []
    for l in range(depth):
        w_main, w_t = _pack_proj_weights(w_in[l], d_model)
        wl = dict(norm1_g=norm1_g[l], w_main=w_main, w_t=w_t,
                  pool_k=cmp_pool_k[l].reshape(CMP_LEN, NSA_KVW), pool_v=cmp_pool_v[l].reshape(CMP_LEN, NSA_KVW),
                  wbd_k=_block_diag(cmp_proj_k[l]), wbd_v=_block_diag(cmp_proj_v[l]),
                  w_up_sb=w_up_sb[l].astype(BF16), w_up_nsa=w_up_nsa[l].astype(BF16), w_out=w_out[l].astype(BF16),
                  norm2_g=norm2_g[l], w_ff_in=w_ff_in[l].astype(BF16), w_ff_out=w_ff_out[l].astype(BF16))
        last = l == depth - 1
        h_p, st_p = _prompt_layer(h_p, wl, norm_f_g, last)
        h_s, st_s = _sample_layer(h_s, l, caches_t, n_pool, state_win_k[l], state_win_v[l], page_table, wl,
                                  norm_f_g, last)
        new_p.append(st_p)
        new_s.append(st_s)
    stk = lambda lst, j: jnp.stack([st[j] for st in lst])
    return (h_p, h_s) + tuple(stk(new_p, j) for j in range(8)) + tuple(stk(new_s, j) for j in range(8))
```

```python
import functools

import numpy as np
import jax
import jax.numpy as jnp
from jax import lax
from jax.experimental import pallas as pl
from jax.experimental.pallas import tpu as pltpu

SB_HEADS = 8
SB_DIM = 64
NSA_HEADS = 8
NSA_GROUPS = 2
NSA_DIM = 64
HEADS_PER_GROUP = NSA_HEADS // NSA_GROUPS
CMP_LEN = 32
CMP_STRIDE = 16
CMP_RATIO = CMP_LEN // CMP_STRIDE
SLC_BLOCK = 64
N_SEL = 16
WINDOW = 512
N_NSA_BRANCH = 3
EPS = 1e-6
NEG = -1e30
FORCE = 1e4
LOWEST = -3e38
LOG2E = 1.4426950408889634

LANES = 128
VMEM_LIMIT = 56 * 1024 * 1024

SB_W = SB_HEADS * SB_DIM
NSA_QW = NSA_HEADS * NSA_DIM
NSA_KVW = NSA_GROUPS * NSA_DIM

F32 = jnp.float32
BF16 = jnp.bfloat16
NT_DIMS = (((1,), (1,)), ((), ()))


def _dot(a, b):
    return jnp.dot(a, b, preferred_element_type=F32)


def _dot_nt(a, b):
    return lax.dot_general(a, b, NT_DIMS, preferred_element_type=F32)


def _split_bf16(x):
    hi = x.astype(BF16)
    lo = (x - hi.astype(F32)).astype(BF16)
    return hi, lo


def _params(*sem):
    return pltpu.CompilerParams(dimension_semantics=sem, vmem_limit_bytes=VMEM_LIMIT)


def _const_spec(shape):
    nd = len(shape)
    return pl.BlockSpec(shape, lambda *_: (0,) * nd, pipeline_mode=pl.Buffered(1))


def _channel_major(t):
    lead = t.shape[:-3]
    n = len(lead)
    t = jnp.transpose(t, tuple(range(n)) + (n + 1, n + 2, n))
    return t.reshape(lead + (t.shape[-3] * t.shape[-2], t.shape[-1]))


def _token_major(t, groups):
    lead = t.shape[:-2]
    n = len(lead)
    t = t.reshape(lead + (groups, t.shape[-2] // groups, t.shape[-1]))
    return jnp.transpose(t, tuple(range(n)) + (n + 2, n, n + 1))


_C_SQ = 0
_C_NQ = _C_SQ + SB_W
_C_CK = _C_NQ + NSA_HEADS * LANES
_C_CV = _C_CK + NSA_KVW
_C_LK = _C_CV + NSA_KVW
_C_WK = _C_LK + NSA_KVW
_C_MG = _C_WK + NSA_KVW
_R_SK = 0
_R_SV = _R_SK + SB_W
_R_CK = _R_SV + SB_W
_R_CV = _R_CK + NSA_KVW
_R_LK = _R_CV + NSA_KVW
_R_LV = _R_LK + NSA_KVW
_R_WK = _R_LV + NSA_KVW
_R_WV = _R_WK + NSA_KVW
_R_NG = _R_WV + NSA_KVW
NG_ROWS = 16
_R_END = _R_NG + NSA_GROUPS * NG_ROWS


def _proj_kernel(x_ref, g_ref, w_ref, wt_ref,
                 sq_o, nq_o, ck_o, cv_o, lkb_o, wkb_o, mg_o,
                 skt_o, sktb_o, svt_o, svtb_o, ckt_o, cvt_o, lkt_o, lvt_o, lvtb_o, wkt_o, wvt_o, wvtb_o, ngt_o,
                 *, d_mg):
    x = x_ref[0]
    ms = jnp.mean(x * x, axis=-1, keepdims=True)
    xn = (x * lax.rsqrt(ms + EPS) * g_ref[...]).astype(BF16)

    def cols(c0, width):
        return _dot(xn, w_ref[:, c0:c0 + width])

    def rows(r0, height):
        return _dot_nt(wt_ref[r0:r0 + height, :], xn)

    sq_o[0] = (cols(_C_SQ, SB_W) * (SB_DIM ** -0.5 * LOG2E)).astype(BF16)
    nq_o[0] = (cols(_C_NQ, NSA_HEADS * LANES) * (NSA_DIM ** -0.5 * LOG2E)).astype(BF16)
    ck_o[0] = cols(_C_CK, NSA_KVW)
    cv_o[0] = cols(_C_CV, NSA_KVW)
    lkb_o[0] = cols(_C_LK, NSA_KVW).astype(BF16)
    wkb_o[0] = cols(_C_WK, NSA_KVW).astype(BF16)
    mg_o[0] = cols(_C_MG, d_mg)
    skt = rows(_R_SK, SB_W)
    skt_o[0] = skt
    sktb_o[0] = skt.astype(BF16)
    svt = rows(_R_SV, SB_W)
    svt_o[0] = svt
    svtb_o[0] = svt.astype(BF16)
    ckt_o[0] = rows(_R_CK, NSA_KVW)
    cvt_o[0] = rows(_R_CV, NSA_KVW)
    lkt_o[0] = rows(_R_LK, NSA_KVW)
    lvt = rows(_R_LV, NSA_KVW)
    lvt_o[0] = lvt
    lvtb_o[0] = lvt.astype(BF16)
    wkt_o[0] = rows(_R_WK, NSA_KVW)
    wvt = rows(_R_WV, NSA_KVW)
    wvt_o[0] = wvt
    wvtb_o[0] = wvt.astype(BF16)
    ngt_o[0] = rows(_R_NG, NSA_GROUPS * NG_ROWS)


def _pack_proj_weights(w_in, d_model):
    splits = np.cumsum([SB_W, SB_W, SB_W, NSA_QW, NSA_KVW, NSA_KVW, NSA_KVW, NSA_KVW, NSA_KVW, NSA_KVW,
                        NSA_HEADS * N_NSA_BRANCH])
    sq, sk, sv, nq, ck, cv, lk, lv, wk, wv, ng, mg = jnp.split(w_in, splits.tolist(), axis=1)
    nq4 = nq.reshape(d_model, NSA_GROUPS, HEADS_PER_GROUP, NSA_DIM)
    nq_pad = jnp.zeros((d_model, NSA_GROUPS, HEADS_PER_GROUP, NSA_GROUPS, NSA_DIM), w_in.dtype)
    for g in range(NSA_GROUPS):
        nq_pad = nq_pad.at[:, g, :, g, :].set(nq4[:, g])
    nq_pad = nq_pad.reshape(d_model, NSA_HEADS * LANES)
    w_main = jnp.concatenate([sq, nq_pad, ck, cv, lk, wk, mg], axis=1).astype(BF16)
    ng3 = ng.reshape(d_model, NSA_GROUPS, HEADS_PER_GROUP * N_NSA_BRANCH)
    ng3 = jnp.pad(ng3, ((0, 0), (0, 0), (0, NG_ROWS - HEADS_PER_GROUP * N_NSA_BRANCH)))
    w_t = jnp.concatenate([sk, sv, ck, cv, lk, lv, wk, wv, ng3.reshape(d_model, NSA_GROUPS * NG_ROWS)],
                          axis=1).T.astype(BF16)
    return w_main, w_t


def _project(x, norm_g, w_main, w_t, tm):
    b, s, d = x.shape
    d_mg = w_main.shape[1] - _C_MG
    row = lambda width: pl.BlockSpec((1, tm, width), lambda bi, i: (bi, i, 0))
    col = lambda rows: pl.BlockSpec((1, rows, tm), lambda bi, i: (bi, 0, i))
    tok = lambda width, dt: jax.ShapeDtypeStruct((b, s, width), dt)
    chn = lambda rows, dt: jax.ShapeDtypeStruct((b, rows, s), dt)
    out_shape = (tok(SB_W, BF16), tok(NSA_HEADS * LANES, BF16), tok(NSA_KVW, F32), tok(NSA_KVW, F32),
                 tok(NSA_KVW, BF16), tok(NSA_KVW, BF16), tok(d_mg, F32),
                 chn(SB_W, F32), chn(SB_W, BF16), chn(SB_W, F32), chn(SB_W, BF16),
                 chn(NSA_KVW, F32), chn(NSA_KVW, F32), chn(NSA_KVW, F32), chn(NSA_KVW, F32), chn(NSA_KVW, BF16),
                 chn(NSA_KVW, F32), chn(NSA_KVW, F32), chn(NSA_KVW, BF16), chn(NSA_GROUPS * NG_ROWS, F32))
    out_specs = (row(SB_W), row(NSA_HEADS * LANES), row(NSA_KVW), row(NSA_KVW), row(NSA_KVW), row(NSA_KVW),
                 row(d_mg),
                 col(SB_W), col(SB_W), col(SB_W), col(SB_W),
                 col(NSA_KVW), col(NSA_KVW), col(NSA_KVW), col(NSA_KVW), col(NSA_KVW),
                 col(NSA_KVW), col(NSA_KVW), col(NSA_KVW), col(NSA_GROUPS * NG_ROWS))
    names = ("sq", "nq", "ck", "cv", "lkb", "wkb", "mg", "skt", "sktb", "svt", "svtb", "ckt", "cvt", "lkt", "lvt",
             "lvtb", "wkt", "wvt", "wvtb", "ngt")
    outs = pl.pallas_call(
        functools.partial(_proj_kernel, d_mg=d_mg),
        grid=(b, s // tm),
        in_specs=[row(d), _const_spec((1, d)), _const_spec(w_main.shape), _const_spec(w_t.shape)],
        out_specs=out_specs,
        out_shape=out_shape,
        compiler_params=_params("parallel", "parallel"),
        name="in_projection",
    )(x, norm_g.reshape(1, d), w_main, w_t)
    return dict(zip(names, outs))


def _compress_kernel(ck_ref, cv_ref, pk_ref, pv_ref, wk_ref, wvt_ref, kc_o, vct_o):
    n_sub = ck_ref.shape[1] // CMP_STRIDE

    def pooled(src, pw_ref):
        pw = pw_ref[...]
        a = jnp.zeros((n_sub, NSA_KVW), F32)
        b = jnp.zeros((n_sub, NSA_KVW), F32)
        for r in range(CMP_STRIDE):
            rows = src[0, pl.ds(r, n_sub, stride=CMP_STRIDE), :]
            a = a + rows * pw[r:r + 1, :]
            b = b + rows * pw[CMP_STRIDE + r:CMP_STRIDE + r + 1, :]
        return a + pltpu.roll(b, n_sub - 1, axis=0)

    kc_o[0] = _dot(pooled(ck_ref, pk_ref).astype(BF16), wk_ref[...]).astype(BF16)
    vct_o[0] = _dot_nt(wvt_ref[...], pooled(cv_ref, pv_ref).astype(BF16)).astype(BF16)


def _block_diag(proj):
    out = jnp.zeros((NSA_KVW, NSA_KVW), proj.dtype)
    for g in range(NSA_GROUPS):
        out = out.at[g * NSA_DIM:(g + 1) * NSA_DIM, g * NSA_DIM:(g + 1) * NSA_DIM].set(proj[g])
    return out


def _compress_prompt(ck, cv, pool_k, pool_v, wbd_k, wbd_v):
    b, s, _ = ck.shape
    n_sub = s // CMP_STRIDE
    kv = pl.BlockSpec((1, s, NSA_KVW), lambda bi: (bi, 0, 0))
    return pl.pallas_call(
        _compress_kernel,
        grid=(b,),
        in_specs=[kv, kv, _const_spec((CMP_LEN, NSA_KVW)), _const_spec((CMP_LEN, NSA_KVW)),
                  _const_spec((NSA_KVW, NSA_KVW)), _const_spec((NSA_KVW, NSA_KVW))],
        out_specs=(pl.BlockSpec((1, n_sub, NSA_KVW), lambda bi: (bi, 0, 0)),
                   pl.BlockSpec((1, NSA_KVW, n_sub), lambda bi: (bi, 0, 0))),
        out_shape=(jax.ShapeDtypeStruct((b, n_sub, NSA_KVW), BF16),
                   jax.ShapeDtypeStruct((b, NSA_KVW, n_sub), BF16)),
        compiler_params=_params("parallel"),
        name="compress_prompt",
    )(ck, cv, pool_k, pool_v, wbd_k.astype(BF16), wbd_v.T.astype(BF16))


def _softplus2(u):
    return jnp.maximum(u, 0.0) + jnp.log2(1.0 + jnp.exp2(-jnp.abs(u)))


def _strict_lower(n):
    return jnp.asarray(np.tril(np.ones((n, n), np.float32), -1), BF16)


def _sb_prompt_kernel(q_ref, kt_ref, vt_ref, tri_ref, o_ref, r_ref, acc_ref, *, t):
    i = pl.program_id(2)
    q = q_ref[0]
    lane = lax.broadcasted_iota(jnp.int32, (t, LANES), 1)
    zero = jnp.zeros_like(q)
    q_heads = (jnp.where(lane < SB_DIM, q, zero), jnp.where(lane >= SB_DIM, q, zero))
    tri = tri_ref[...]
    causal = lax.broadcasted_iota(jnp.int32, (t, t), 1) < lax.broadcasted_iota(jnp.int32, (t, t), 0)

    def group(tile_ids, first_is_diag):
        for h, qx in enumerate(q_heads):
            parts = []
            for n, j in enumerate(tile_ids):
                off = pl.multiple_of(j * t, t)
                z = _dot(qx, kt_ref[0, :, pl.ds(off, t)])
                sp = _softplus2(z)
                log_keep = -sp
                diag = first_is_diag and n == 0
                if diag:
                    log_keep = jnp.where(causal, log_keep, 0.0)
                local = _dot(log_keep.astype(BF16), tri)
                parts.append((off, diag, z - sp + local, jnp.sum(log_keep, axis=-1, keepdims=True)))
            r = r_ref[h]
            acc = acc_ref[h]
            for off, diag, arg, total in parts:
                a = jnp.exp2(arg + r)
                if diag:
                    a = jnp.where(causal, a, 0.0)
                acc = acc + _dot_nt(a.astype(BF16), vt_ref[0, :, pl.ds(off, t)])
                r = r + total
            r_ref[h] = r
            acc_ref[h] = acc

    r_ref[...] = jnp.zeros_like(r_ref)
    acc_ref[...] = jnp.zeros_like(acc_ref)
    odd = i & 1

    @pl.when(odd == 1)
    def _():
        group((i, i - 1), True)

    @pl.when(odd == 0)
    def _():
        group((i,), True)

    top = i - 1 - odd

    def pair(p, carry):
        group((top - 2 * p, top - 2 * p - 1), False)
        return carry

    lax.fori_loop(0, i // 2, pair, 0)
    o_ref[0] = jnp.where(lane < SB_DIM, acc_ref[0], acc_ref[1]).astype(o_ref.dtype)


def _sb_prompt(sq, skt, svt, t):
    b, s, w = sq.shape
    qo = pl.BlockSpec((1, t, LANES), lambda bi, hp, i: (bi, i, hp))
    kv = pl.BlockSpec((1, LANES, s), lambda bi, hp, i: (bi, hp, 0))
    return pl.pallas_call(
        functools.partial(_sb_prompt_kernel, t=t),
        grid=(b, w // LANES, s // t),
        in_specs=[qo, kv, kv, _const_spec((t, t))],
        out_specs=qo,
        out_shape=jax.ShapeDtypeStruct((b, s, w), BF16),
        scratch_shapes=[pltpu.VMEM((2, t, 1), F32), pltpu.VMEM((2, t, LANES), F32)],
        compiler_params=_params("parallel", "parallel", "arbitrary"),
        name="sb_prompt",
    )(sq, skt, svt, _strict_lower(t))


def _sb_sample_kernel(pt_ref, q_ref, *refs, n_pc):
    k_refs = refs[:n_pc]
    v_refs = refs[n_pc:2 * n_pc]
    tri_ref = refs[2 * n_pc]
    o_ref = refs[2 * n_pc + 1]
    r_ref, acc_ref = refs[2 * n_pc + 2:]
    c = pl.program_id(1)
    page = k_refs[0].shape[-1]

    @pl.when(c == 0)
    def _():
        r_ref[...] = jnp.zeros_like(r_ref)
        acc_ref[...] = jnp.zeros_like(acc_ref)

    q_col = q_ref[0].reshape(SB_HEADS, SB_DIM, page)
    scores = []
    for ii in range(n_pc):
        kt = k_refs[ii][0].reshape(SB_HEADS, SB_DIM, page)
        scores.append(jnp.sum(kt * q_col, axis=1))
    z = jnp.concatenate(scores, axis=0)
    sp = _softplus2(z)
    log_keep = -sp
    hi, lo = _split_bf16(log_keep)
    both = _dot(jnp.concatenate([hi, lo], axis=0), tri_ref[...])
    arg = z - sp + both[:n_pc * SB_HEADS] + both[n_pc * SB_HEADS:]
    totals = jnp.sum(log_keep, axis=-1, keepdims=True)
    r = r_ref[:, 0:1]
    acc = acc_ref[...]
    for ii in range(n_pc):
        rows = slice(ii * SB_HEADS, (ii + 1) * SB_HEADS)
        a = jnp.exp2(arg[rows] + r)
        vt = v_refs[ii][0].reshape(SB_HEADS, SB_DIM, page)
        acc = acc + vt * a[:, None, :]
        r = r + totals[rows]
    r_ref[...] = jnp.broadcast_to(r, r_ref.shape)
    acc_ref[...] = acc

    @pl.when(c == pl.num_programs(1) - 1)
    def _():
        o_ref[0] = jnp.sum(acc, axis=-1, keepdims=True).reshape(SB_W, 1)


def _sb_sample(sq_s, cache_kt, cache_vt, pt_flat, layer, n_pool, n_pages, n_pc):
    db = sq_s.shape[0]
    page = cache_kt.shape[-1]
    n_chunks = n_pages // n_pc

    def page_spec(ii):
        def imap(bi, c, pt):
            p = n_pages - 1 - (c * n_pc + ii)
            return (layer * n_pool + jnp.clip(pt[bi * n_pages + p], 0, n_pool - 1), 0, 0)
        return pl.BlockSpec((1, SB_W, page), imap)

    specs = [pl.BlockSpec((1, SB_W, page), lambda bi, c, pt: (bi, 0, 0))]
    specs += [page_spec(ii) for ii in range(n_pc)] * 2
    specs += [pl.BlockSpec((page, page), lambda bi, c, pt: (0, 0))]
    grid_spec = pltpu.PrefetchScalarGridSpec(
        num_scalar_prefetch=1, grid=(db, n_chunks), in_specs=specs,
        out_specs=pl.BlockSpec((1, SB_W, 1), lambda bi, c, pt: (bi, 0, 0)),
        scratch_shapes=[pltpu.VMEM((SB_HEADS, LANES), F32), pltpu.VMEM((SB_HEADS, SB_DIM, page), F32)])
    q_lanes = jnp.broadcast_to(sq_s.astype(F32)[:, :, None], (db, SB_W, page))
    out = pl.pallas_call(
        functools.partial(_sb_sample_kernel, n_pc=n_pc),
        grid_spec=grid_spec,
        out_shape=jax.ShapeDtypeStruct((db, SB_W, 1), F32),
        compiler_params=_params("parallel", "arbitrary"),
        name="sb_sample",
    )(pt_flat, q_lanes, *([cache_kt] * n_pc), *([cache_vt] * n_pc), _strict_lower(page))
    return out.reshape(db, SB_W)


def _rank_desc(v):
    n = v.shape[0]
    sub = lax.broadcasted_iota(jnp.int32, (8, v.shape[1]), 0)
    cnt = jnp.zeros(v.shape, jnp.int32)
    for m in range(n):
        rowv = v[m:m + 1, :]
        g0 = (m // 8) * 8
        parts = []
        if g0 > 0:
            parts.append(jnp.where(rowv > v[:g0], 1, 0))
        blk = v[g0:g0 + 8]
        parts.append(jnp.where(sub > (m % 8), jnp.where(rowv >= blk, 1, 0), jnp.where(rowv > blk, 1, 0)))
        if g0 + 8 < n:
            parts.append(jnp.where(rowv >= v[g0 + 8:], 1, 0))
        cnt = cnt + (parts[0] if len(parts) == 1 else jnp.concatenate(parts, axis=0))
    return cnt


def _cmp_to_slc_t(n_cmp, n_slc, n_cmp_pad, n_slc_pad):
    c_start = np.arange(n_cmp) * CMP_STRIDE
    s_start = np.arange(n_slc) * SLC_BLOCK
    inter = (np.minimum(c_start[None, :] + CMP_LEN, s_start[:, None] + SLC_BLOCK)
             - np.maximum(c_start[None, :], s_start[:, None]))
    ov = np.zeros((n_slc_pad, n_cmp_pad), np.float32)
    ov[:n_slc, :n_cmp] = np.clip(inter, 0, None) / CMP_LEN
    return ov


def _alibi_slopes():
    return (2.0 ** (-8.0 * (np.arange(NSA_HEADS) + 1) / NSA_HEADS) * LOG2E).astype(np.float32)


def _nsa_prompt_kernel(q_ref, lk_ref, lvt_ref, wk_ref, wvt_ref, kc_ref, vct_ref, ngt_ref, slope_ref, ovt_ref,
                       o_ref, sel_ref, bias_ref, *, tq, tks, tkw, n_sel):
    i = pl.program_id(2)
    nh = HEADS_PER_GROUP
    w = nh * tq
    q0 = i * tq
    qs = jnp.concatenate([q_ref[0, :, h * LANES:(h + 1) * LANES] for h in range(nh)], axis=0)
    lane = lax.broadcasted_iota(jnp.int32, (1, w), 1)
    qpos = q0 + (lane & (tq - 1))
    slope = slope_ref[0]

    kc = kc_ref[0]
    n_c = kc.shape[0]
    c_end = lax.broadcasted_iota(jnp.int32, (n_c, 1), 0) * CMP_STRIDE + (CMP_LEN - 1)
    dist = qpos - c_end
    valid = dist >= 0
    s = _dot_nt(kc, qs) - slope * dist.astype(F32)
    sm = jnp.where(valid, s, NEG)
    m = jnp.max(sm, axis=0, keepdims=True)
    e = jnp.where(valid, jnp.exp2(sm - m), 0.0)
    l = jnp.sum(e, axis=0, keepdims=True)
    p = e * (1.0 / jnp.where(l > 0.0, l, 1.0))
    o_cmp = _dot(vct_ref[0], p.astype(BF16))

    p_sum = p[:, 0:tq]
    for h in range(1, nh):
        p_sum = p_sum + p[:, h * tq:(h + 1) * tq]
    hi, lo = _split_bf16(p_sum)
    ovt = ovt_ref[...]
    imp = _dot(ovt, hi) + _dot(ovt, lo)
    n_slc = imp.shape[0]
    blk = lax.broadcasted_iota(jnp.int32, (n_slc, tq), 0)
    cur = (q0 + lax.broadcasted_iota(jnp.int32, (1, tq), 1)) // SLC_BLOCK
    forced = (blk == 0) | (blk == cur) | (blk == cur - 1)
    imp = jnp.where(forced, FORCE, jnp.where(blk <= cur, imp, -FORCE))
    sel_ref[...] = jnp.where(_rank_desc(imp) < n_sel, 0.0, NEG)

    bias_ref[...] = slope * lax.broadcasted_iota(jnp.int32, (tks, 1), 0).astype(F32)

    def sel_tile(j, carry, causal):
        m_old, l_old, acc = carry
        off = pl.multiple_of(j * tks, tks)
        rows = []
        for bb in range(tks // SLC_BLOCK):
            r = sel_ref[pl.ds(j * (tks // SLC_BLOCK) + bb, 1), :]
            rows.append(jnp.broadcast_to(r, (SLC_BLOCK, tq)))
        picked = jnp.concatenate(rows, axis=0)
        picked = jnp.concatenate([picked] * nh, axis=1)
        sm = _dot_nt(lk_ref[0, pl.ds(off, tks), :], qs) + bias_ref[...] + picked
        if causal:
            t_pos = off + lax.broadcasted_iota(jnp.int32, (tks, 1), 0)
            sm = jnp.where(qpos >= t_pos, sm, NEG)
        shift = slope * off.astype(F32)
        m_new = jnp.maximum(m_old, jnp.max(sm, axis=0, keepdims=True) + shift)
        alpha = jnp.exp2(m_old - m_new)
        p_t = jnp.exp2(sm - (m_new - shift))
        l_new = alpha * l_old + jnp.sum(p_t, axis=0, keepdims=True)
        acc = alpha * acc + _dot(lvt_ref[0, :, pl.ds(off, tks)], p_t.astype(BF16))
        return m_new, l_new, acc

    init = (jnp.full((1, w), NEG, F32), jnp.zeros((1, w), F32), jnp.zeros((NSA_DIM, w), F32))
    n_full = q0 // tks
    carry = lax.fori_loop(0, n_full, lambda j, c: sel_tile(j, c, False), init)
    _, l_sel, acc_sel = sel_tile(n_full, carry, True)

    win_scores = []
    for d in range(WINDOW // tkw + 1):
        jt = i - d
        off = pl.multiple_of(jnp.maximum(jt, 0) * tkw, tkw)
        limit = jnp.where(jt >= 0, WINDOW, -1)
        dist_t = qpos - (off + lax.broadcasted_iota(jnp.int32, (tkw, 1), 0))
        valid_t = jnp.where(dist_t <= limit, dist_t, -1) >= 0
        s_t = _dot_nt(wk_ref[0, pl.ds(off, tkw), :], qs) - slope * dist_t.astype(F32)
        win_scores.append((off, jnp.where(valid_t, s_t, NEG)))
    m_win = jnp.max(win_scores[0][1], axis=0, keepdims=True)
    for _, sm_t in win_scores[1:]:
        m_win = jnp.maximum(m_win, jnp.max(sm_t, axis=0, keepdims=True))
    l_win = jnp.zeros((1, w), F32)
    acc_win = jnp.zeros((NSA_DIM, w), F32)
    for off, sm_t in win_scores:
        p_t = jnp.exp2(sm_t - m_win)
        l_win = l_win + jnp.sum(p_t, axis=0, keepdims=True)
        acc_win = acc_win + _dot(wvt_ref[0, :, pl.ds(off, tkw)], p_t.astype(BF16))

    gates = jax.nn.sigmoid(ngt_ref[0])
    o_sel = acc_sel * (1.0 / l_sel)
    o_win = acc_win * (1.0 / l_win)
    heads = []
    for h in range(nh):
        hs = slice(h * tq, (h + 1) * tq)
        g0 = gates[h * N_NSA_BRANCH + 0:h * N_NSA_BRANCH + 1, :]
        g1 = gates[h * N_NSA_BRANCH + 1:h * N_NSA_BRANCH + 2, :]
        g2 = gates[h * N_NSA_BRANCH + 2:h * N_NSA_BRANCH + 3, :]
        heads.append(g0 * o_cmp[:, hs] + g1 * o_sel[:, hs] + g2 * o_win[:, hs])
    for hp in range(nh // 2):
        pair = jnp.concatenate([heads[2 * hp], heads[2 * hp + 1]], axis=0)
        o_ref[0, :, hp * LANES:(hp + 1) * LANES] = pair.T.astype(o_ref.dtype)


def _nsa_prompt(nq_pad, lk_b, lvt, wk_b, wvt, kc, vct, ngt, tq, tks):
    b, s, _ = nq_pad.shape
    assert tq == LANES and s % tks == 0 and tks % SLC_BLOCK == 0
    n_c = kc.shape[1]
    n_slc = s // SLC_BLOCK
    n_cmp = n_c - CMP_RATIO + 1
    w = HEADS_PER_GROUP * tq
    slopes = np.repeat(_alibi_slopes().reshape(NSA_GROUPS, HEADS_PER_GROUP), tq, axis=1).reshape(NSA_GROUPS, 1, w)
    ovt = jnp.asarray(_cmp_to_slc_t(n_cmp, n_slc, n_c, n_slc), BF16)
    full = lambda rows, lanes: pl.BlockSpec((1, rows, lanes), lambda bi, g, i: (bi, 0, 0))
    grp = lambda rows, lanes: pl.BlockSpec((1, rows, lanes), lambda bi, g, i: (bi, g, 0))
    return pl.pallas_call(
        functools.partial(_nsa_prompt_kernel, tq=tq, tks=tks, tkw=tq, n_sel=min(N_SEL, n_slc)),
        grid=(b, NSA_GROUPS, s // tq),
        in_specs=[pl.BlockSpec((1, tq, HEADS_PER_GROUP * LANES), lambda bi, g, i: (bi, i, g)),
                  full(s, NSA_KVW), grp(NSA_DIM, s), full(s, NSA_KVW), grp(NSA_DIM, s),
                  full(n_c, NSA_KVW), grp(NSA_DIM, n_c),
                  pl.BlockSpec((1, NG_ROWS, tq), lambda bi, g, i: (bi, g, i)),
                  pl.BlockSpec((1, 1, w), lambda bi, g, i: (g, 0, 0)),
                  _const_spec((n_slc, n_c))],
        out_specs=pl.BlockSpec((1, tq, HEADS_PER_GROUP * NSA_DIM), lambda bi, g, i: (bi, i, g)),
        out_shape=jax.ShapeDtypeStruct((b, s, NSA_QW), BF16),
        scratch_shapes=[pltpu.VMEM((n_slc, tq), F32), pltpu.VMEM((tks, w), F32)],
        compiler_params=_params("parallel", "parallel", "arbitrary"),
        name="nsa_prompt",
    )(nq_pad, lk_b, lvt, wk_b, wvt, kc, vct, ngt, jnp.asarray(slopes), ovt)


def _nsa_cmp_sample_kernel(pt_ref, q_ref, *refs, n_pc, past):
    k_refs = refs[:n_pc]
    v_refs = refs[n_pc:2 * n_pc]
    pk0_ref, pk1_ref, pv0_ref, pv1_ref, seg_ref, wkt_ref, wvt_ref, slope_ref, ov_ref = refs[2 * n_pc:2 * n_pc + 9]
    oc_o, imp_o = refs[2 * n_pc + 9:2 * n_pc + 11]
    ak_ref, bk_ref, av_ref, bv_ref = refs[2 * n_pc + 11:]
    c = pl.program_id(1)
    page = k_refs[0].shape[-1]
    m_chunk = n_pc * (page // CMP_STRIDE)

    weights = (pk0_ref[...], pk1_ref[...], pv0_ref[...], pv1_ref[...])
    pooled = jnp.zeros((4 * NSA_KVW, m_chunk), F32)
    for pair in range(n_pc // 2):
        xs = (k_refs[2 * pair][0], k_refs[2 * pair + 1][0], v_refs[2 * pair][0], v_refs[2 * pair + 1][0])
        lhs = jnp.concatenate(
            [jnp.concatenate([xs[2 * (n // 2)] * wgt, xs[2 * (n // 2) + 1] * wgt], axis=1)
             for n, wgt in enumerate(weights)], axis=0).astype(BF16)
        pooled = pooled + _dot(lhs, seg_ref[pair])
    off = pl.multiple_of(c * m_chunk, m_chunk)
    for n, ref in enumerate((ak_ref, bk_ref, av_ref, bv_ref)):
        ref[:, pl.ds(off, m_chunk)] = pooled[n * NSA_KVW:(n + 1) * NSA_KVW]

    @pl.when(c == pl.num_programs(1) - 1)
    def _():
        n_sub = ak_ref.shape[1]
        pooled_k = ak_ref[...] + pltpu.roll(bk_ref[...], n_sub - 1, axis=1)
        pooled_v = av_ref[...] + pltpu.roll(bv_ref[...], n_sub - 1, axis=1)
        kct = _dot(wkt_ref[...], pooled_k.astype(BF16)).astype(BF16)
        vct = _dot(wvt_ref[...], pooled_v.astype(BF16)).astype(BF16)
        qn = q_ref[0].astype(BF16)
        c_end = lax.broadcasted_iota(jnp.int32, (1, n_sub), 1) * CMP_STRIDE + (CMP_LEN - 1)
        dist = past - c_end
        valid = dist >= 0
        s = _dot(qn, kct) - slope_ref[:, 0:1] * dist.astype(F32)
        sm = jnp.where(valid, s, NEG)
        m = jnp.max(sm, axis=-1, keepdims=True)
        e = jnp.where(valid, jnp.exp2(sm - m), 0.0)
        l = jnp.sum(e, axis=-1, keepdims=True)
        p = e * (1.0 / jnp.where(l > 0.0, l, 1.0))
        oc_o[0] = _dot_nt(p.astype(BF16), vct)
        sums = [jnp.sum(p[g * HEADS_PER_GROUP:(g + 1) * HEADS_PER_GROUP], axis=0, keepdims=True)
                for g in range(NSA_GROUPS)]
        sums.append(jnp.zeros((NSA_HEADS - NSA_GROUPS, n_sub), F32))
        hi, lo = _split_bf16(jnp.concatenate(sums, axis=0))
        ov = ov_ref[...]
        imp_o[0] = _dot(hi, ov) + _dot(lo, ov)


def _nsa_cmp_sample(qn_s, cache_kt, cache_vt, pt_flat, layer, n_pool, n_pages, n_pc, pool_k, pool_v, wbd_k, wbd_v,
                    n_slc_pad):
    db = qn_s.shape[0]
    page = cache_kt.shape[-1]
    past = n_pages * page
    n_sub = past // CMP_STRIDE
    n_cmp = n_sub - CMP_RATIO + 1
    n_slc = -(-(past + 1) // SLC_BLOCK)
    sub_per_page = page // CMP_STRIDE
    m_chunk = n_pc * sub_per_page
    assert n_pc % 2 == 0 and m_chunk % LANES == 0
    ov = jnp.asarray(_cmp_to_slc_t(n_cmp, n_slc, n_sub, n_slc_pad).T, BF16)
    slopes = jnp.asarray(np.repeat(_alibi_slopes()[:, None], LANES, axis=1))
    t_idx = np.arange(2 * page)
    seg = np.zeros((n_pc // 2, 2 * page, m_chunk), np.float32)
    for pair in range(n_pc // 2):
        seg[pair, t_idx, 2 * sub_per_page * pair + t_idx // CMP_STRIDE] = 1.0
    tile_w = lambda pw, half: jnp.tile(pw[half * CMP_STRIDE:(half + 1) * CMP_STRIDE].T, (1, sub_per_page))

    def page_spec(ii):
        return pl.BlockSpec((1, NSA_KVW, page),
                            lambda bi, c, pt: (layer * n_pool
                                               + jnp.clip(pt[bi * n_pages + c * n_pc + ii], 0, n_pool - 1), 0, 0))

    cst = lambda shape: pl.BlockSpec(shape, lambda bi, c, pt: (0,) * len(shape))
    specs = [pl.BlockSpec((1, NSA_HEADS, LANES), lambda bi, c, pt: (bi, 0, 0))]
    specs += [page_spec(ii) for ii in range(n_pc)] * 2
    specs += [cst((NSA_KVW, page))] * 4
    specs += [cst(seg.shape), cst((NSA_KVW, NSA_KVW)), cst((NSA_KVW, NSA_KVW)),
              cst((NSA_HEADS, LANES)), cst((n_sub, n_slc_pad))]
    out_blk = lambda lanes: pl.BlockSpec((1, NSA_HEADS, lanes), lambda bi, c, pt: (bi, 0, 0))
    grid_spec = pltpu.PrefetchScalarGridSpec(
        num_scalar_prefetch=1, grid=(db, n_pages // n_pc), in_specs=specs,
        out_specs=(out_blk(LANES), out_blk(n_slc_pad)),
        scratch_shapes=[pltpu.VMEM((NSA_KVW, n_sub), F32)] * 4)
    return pl.pallas_call(
        functools.partial(_nsa_cmp_sample_kernel, n_pc=n_pc, past=past),
        grid_spec=grid_spec,
        out_shape=(jax.ShapeDtypeStruct((db, NSA_HEADS, LANES), F32),
                   jax.ShapeDtypeStruct((db, NSA_HEADS, n_slc_pad), F32)),
        compiler_params=_params("parallel", "arbitrary"),
        name="nsa_cmp_sample",
    )(pt_flat, qn_s, *([cache_kt] * n_pc), *([cache_vt] * n_pc),
      tile_w(pool_k, 0), tile_w(pool_k, 1), tile_w(pool_v, 0), tile_w(pool_v, 1), jnp.asarray(seg, BF16),
      wbd_k.T.astype(BF16), wbd_v.T.astype(BF16), slopes, ov)


def _topk_sample_kernel(imp_ref, idx_o, *, n_slc, cur, n_sel):
    imp = imp_ref[...]
    blk = lax.broadcasted_iota(jnp.int32, imp.shape, 0)
    forced = (blk == 0) | (blk == cur) | (blk == cur - 1)
    imp = jnp.where(forced, FORCE, jnp.where(blk <= cur, imp, -FORCE))
    imp = jnp.where(blk < n_slc, imp, LOWEST)
    rank = _rank_desc(imp)
    rows = [jnp.sum(jnp.where(rank == r, blk, 0), axis=0, keepdims=True) for r in range(n_sel)]
    idx_o[...] = jnp.concatenate(rows, axis=0)


def _topk_sample(imp_t, n_slc, cur, n_sel):
    n_pad, cols = imp_t.shape
    return pl.pallas_call(
        functools.partial(_topk_sample_kernel, n_slc=n_slc, cur=cur, n_sel=n_sel),
        grid=(1,),
        in_specs=[_const_spec((n_pad, cols))],
        out_specs=_const_spec((n_sel, cols)),
        out_shape=jax.ShapeDtypeStruct((n_sel, cols), jnp.int32),
        compiler_params=_params("arbitrary"),
        name="topk_sample",
    )(imp_t)


def _nsa_sel_sample_kernel(pt_ref, idx_ref, q_ref, *refs, n_sel, n_cached, past, n_buf):
    n_blk = NSA_GROUPS * n_sel
    k_refs = refs[:n_blk]
    v_refs = refs[n_blk:2 * n_blk]
    wink_ref, winv_ref, new_ref, ng_ref, oc_ref, slope_ref, o_ref = refs[2 * n_blk:]
    bi = pl.program_id(0)
    page = k_refs[0].shape[-1]
    halves = page // SLC_BLOCK
    qn = q_ref[0]
    qb = qn.astype(BF16)
    rowi = lax.broadcasted_iota(jnp.int32, (NSA_HEADS, 1), 0)
    in_g0 = rowi < HEADS_PER_GROUP
    slope = slope_ref[:, 0:1]
    lk_new, lv_new, wk_new, wv_new = (new_ref[0, r:r + 1, :] for r in range(4))

    def new_score(k_row):
        return jnp.sum(qn * k_row, axis=-1, keepdims=True)

    width = n_sel * page
    lane = lax.broadcasted_iota(jnp.int32, (1, width), 1)
    slot = lane // page
    scores, values, blk_rows, has_new = [], [], [], []
    for g in range(NSA_GROUPS):
        ids = [idx_ref[(bi * NSA_GROUPS + g) * n_sel + r] for r in range(n_sel)]
        scores.append(jnp.concatenate(
            [_dot(qb, k_refs[g * n_sel + r][0].astype(BF16)) for r in range(n_sel)], axis=1))
        values.append(jnp.concatenate([v_refs[g * n_sel + r][0] for r in range(n_sel)], axis=1).astype(BF16))
        row = jnp.zeros((1, width), jnp.int32)
        flag = jnp.int32(0)
        for r in range(n_sel):
            row = jnp.where(slot == r, ids[r], row)
            flag = jnp.maximum(flag, jnp.where(ids[r] >= n_cached, 1, 0))
        blk_rows.append(row)
        has_new.append(flag)
    s = jnp.where(in_g0, scores[0], scores[1])
    blk = jnp.where(in_g0, blk_rows[0], blk_rows[1])
    page_blk = jnp.minimum(blk, n_cached - 1)
    tok = (page_blk // halves) * page + (lane & (page - 1))
    dist = past - tok
    valid = jnp.where(blk < n_cached, tok // SLC_BLOCK, -1) == blk
    s = s - slope * dist.astype(F32)
    sm = jnp.where(valid, s, NEG)
    new_on = jnp.where(in_g0, has_new[0], has_new[1]) > 0
    s_new = jnp.where(new_on, new_score(lk_new), NEG)
    m = jnp.maximum(jnp.max(sm, axis=-1, keepdims=True), s_new)
    e = jnp.where(valid, jnp.exp2(sm - m), 0.0)
    e_new = jnp.where(new_on, jnp.exp2(s_new - m), 0.0)
    l = jnp.sum(e, axis=-1, keepdims=True) + e_new
    eb = e.astype(BF16)
    o_sel = jnp.where(in_g0, _dot_nt(eb, values[0]), _dot_nt(eb, values[1])) + e_new * lv_new
    o_sel = o_sel * (1.0 / l)

    pos = lax.broadcasted_iota(jnp.int32, (1, n_buf), 1)
    dist_w = n_buf - pos
    valid_w = jnp.where(dist_w <= WINDOW, past - dist_w, -1) >= 0
    s_w = _dot(qb, wink_ref[0].astype(BF16)) - slope * dist_w.astype(F32)
    sm_w = jnp.where(valid_w, s_w, NEG)
    s_wn = new_score(wk_new)
    m_w = jnp.maximum(jnp.max(sm_w, axis=-1, keepdims=True), s_wn)
    e_w = jnp.where(valid_w, jnp.exp2(sm_w - m_w), 0.0)
    e_wn = jnp.exp2(s_wn - m_w)
    l_w = jnp.sum(e_w, axis=-1, keepdims=True) + e_wn
    o_win = (_dot_nt(e_w.astype(BF16), winv_ref[0].astype(BF16)) + e_wn * wv_new) * (1.0 / l_w)

    gates = jax.nn.sigmoid(ng_ref[0])
    o = gates[:, 0:1] * oc_ref[0] + gates[:, 1:2] * o_sel + gates[:, 2:3] * o_win
    lane_g = lax.broadcasted_iota(jnp.int32, (NSA_HEADS, LANES), 1) // NSA_DIM
    o_ref[0] = jnp.where(lane_g == rowi // HEADS_PER_GROUP, o, 0.0)


def _nsa_sel_sample(qn_s, cache_kt, cache_vt, win_kt, win_vt, new_rows, ng_s, o_cmp, pt_flat, idx_flat, layer,
                    n_pool, n_pages):
    db = qn_s.shape[0]
    page = cache_kt.shape[-1]
    past = n_pages * page
    halves = page // SLC_BLOCK
    n_cached = past // SLC_BLOCK
    n_sel = idx_flat.shape[0] // (db * NSA_GROUPS)
    n_buf = win_kt.shape[-1]
    slopes = jnp.asarray(np.repeat(_alibi_slopes()[:, None], LANES, axis=1))

    def blk_spec(g, r):
        def imap(bi, pt, idx):
            n = jnp.clip(idx[(bi * NSA_GROUPS + g) * n_sel + r], 0, n_cached - 1)
            return (layer * n_pool + jnp.clip(pt[bi * n_pages + n // halves], 0, n_pool - 1), 0, 0)
        return pl.BlockSpec((1, NSA_KVW, page), imap)

    per_req = lambda rows, lanes: pl.BlockSpec((1, rows, lanes), lambda bi, pt, idx: (bi, 0, 0))
    blocks = [blk_spec(g, r) for g in range(NSA_GROUPS) for r in range(n_sel)]
    specs = [per_req(NSA_HEADS, LANES)] + blocks + blocks
    specs += [per_req(NSA_KVW, n_buf), per_req(NSA_KVW, n_buf), per_req(8, NSA_KVW),
              per_req(NSA_HEADS, LANES), per_req(NSA_HEADS, LANES),
              pl.BlockSpec((NSA_HEADS, LANES), lambda bi, pt, idx: (0, 0))]
    grid_spec = pltpu.PrefetchScalarGridSpec(
        num_scalar_prefetch=2, grid=(db,), in_specs=specs, out_specs=per_req(NSA_HEADS, LANES))
    n_blk = len(blocks)
    return pl.pallas_call(
        functools.partial(_nsa_sel_sample_kernel, n_sel=n_sel, n_cached=n_cached, past=past, n_buf=n_buf),
        grid_spec=grid_spec,
        out_shape=jax.ShapeDtypeStruct((db, NSA_HEADS, LANES), F32),
        compiler_params=_params("parallel"),
        name="nsa_sel_sample",
    )(pt_flat, idx_flat, qn_s, *([cache_kt] * n_blk), *([cache_vt] * n_blk), win_kt, win_vt, new_rows, ng_s, o_cmp,
      slopes)


def _merge_kernel(x_ref, osb_ref, onsa_ref, mg_ref, wsb_ref, wnsa_ref, wout_ref, g2_ref, wfi_ref, wfo_ref, gf_ref,
                  y_ref, *, final_norm):
    d = x_ref.shape[-1]
    gate = jax.nn.sigmoid(mg_ref[0])
    mixed = gate[:, 0:d] * _dot(osb_ref[0], wsb_ref[...]) + gate[:, d:2 * d] * _dot(onsa_ref[0], wnsa_ref[...])
    x1 = x_ref[0] + _dot(mixed.astype(BF16), wout_ref[...])
    ms = jnp.mean(x1 * x1, axis=-1, keepdims=True)
    hn = (x1 * lax.rsqrt(ms + EPS) * g2_ref[...]).astype(BF16)
    u = jnp.maximum(_dot(hn, wfi_ref[...]), 0.0)
    x2 = x1 + _dot((u * u).astype(BF16), wfo_ref[...])
    if final_norm:
        ms2 = jnp.mean(x2 * x2, axis=-1, keepdims=True)
        x2 = x2 * lax.rsqrt(ms2 + EPS) * gf_ref[...]
    y_ref[0] = x2


def _merge(x, o_sb, o_nsa, mg, w_up_sb, w_up_nsa, w_out, norm2_g, w_ff_in, w_ff_out, norm_f_g, final_norm, tm):
    b, s, d = x.shape
    d_ff = w_ff_in.shape[1]
    row = lambda width: pl.BlockSpec((1, tm, width), lambda bi, i: (bi, i, 0))
    return pl.pallas_call(
        functools.partial(_merge_kernel, final_norm=final_norm),
        grid=(b, s // tm),
        in_specs=[row(d), row(SB_W), row(NSA_QW), row(2 * d),
                  _const_spec((SB_W, d)), _const_spec((NSA_QW, d)), _const_spec((d, d)), _const_spec((1, d)),
                  _const_spec((d, d_ff)), _const_spec((d_ff, d)), _const_spec((1, d))],
        out_specs=row(d),
        out_shape=jax.ShapeDtypeStruct((b, s, d), F32),
        compiler_params=_params("parallel", "parallel"),
        name="merge_mlp",
    )(x, o_sb, o_nsa, mg, w_up_sb, w_up_nsa, w_out, norm2_g.reshape(1, d), w_ff_in, w_ff_out, norm_f_g.reshape(1, d))


def _row_tile(n, target):
    t = min(n, target)
    assert n % t == 0
    return t


def _prompt_layer(x, wl, norm_f_g, final_norm):
    b, s, d = x.shape
    pr = _project(x, wl["norm1_g"], wl["w_main"], wl["w_t"], _row_tile(s, 256))
    kc, vct = _compress_prompt(pr["ck"], pr["cv"], wl["pool_k"], wl["pool_v"], wl["wbd_k"], wl["wbd_v"])
    o_sb = _sb_prompt(pr["sq"], pr["sktb"], pr["svtb"], _row_tile(s, 256))
    o_nsa = _nsa_prompt(pr["nq"], pr["lkb"], pr["lvtb"], pr["wkb"], pr["wvtb"], kc, vct, pr["ngt"], LANES, 512)
    y = _merge(x, o_sb, o_nsa, pr["mg"], wl["w_up_sb"], wl["w_up_nsa"], wl["w_out"], wl["norm2_g"],
               wl["w_ff_in"], wl["w_ff_out"], norm_f_g, final_norm, _row_tile(s, 256))
    n_keep = min(WINDOW, s)
    state = (_token_major(pr["skt"], SB_HEADS), _token_major(pr["svt"], SB_HEADS),
             _token_major(pr["ckt"], NSA_GROUPS), _token_major(pr["cvt"], NSA_GROUPS),
             _token_major(pr["lkt"], NSA_GROUPS), _token_major(pr["lvt"], NSA_GROUPS),
             _token_major(pr["wkt"][:, :, s - n_keep:], NSA_GROUPS),
             _token_major(pr["wvt"][:, :, s - n_keep:], NSA_GROUPS))
    return y, state


def _sample_layer(x, layer, caches_t, n_pool, win_k, win_v, page_table, wl, norm_f_g, final_norm):
    c_sb_k, c_sb_v, c_cmp_k, c_cmp_v, c_slc_k, c_slc_v = caches_t
    db, n_new, d = x.shape
    assert n_new == 1
    n_pages = page_table.shape[1]
    page = c_sb_k.shape[-1]
    past = n_pages * page
    pt_flat = page_table.reshape(-1).astype(jnp.int32)
    pr = _project(x.reshape(1, db, d), wl["norm1_g"], wl["w_main"], wl["w_t"], db)
    n_pc = 16 if n_pages % 16 == 0 else 2
    o_sb = _sb_sample(pr["sq"][0], c_sb_k, c_sb_v, pt_flat, layer, n_pool, n_pages, n_pc // 2)

    qn_s = pr["nq"][0].astype(F32).reshape(db, NSA_HEADS, LANES)
    n_slc = -(-(past + n_new) // SLC_BLOCK)
    cur = past // SLC_BLOCK
    n_slc_pad = -(-n_slc // LANES) * LANES
    o_cmp, imp = _nsa_cmp_sample(qn_s, c_cmp_k, c_cmp_v, pt_flat, layer, n_pool, n_pages, n_pc,
                                 wl["pool_k"], wl["pool_v"], wl["wbd_k"], wl["wbd_v"], n_slc_pad)
    n_rank = -(-n_slc // 8) * 8
    imp_t = imp[:, :NSA_GROUPS, :n_rank].reshape(db * NSA_GROUPS, n_rank).T
    n_sel = min(N_SEL, n_slc)
    idx = _topk_sample(imp_t, n_slc, cur, n_sel)
    idx_flat = idx.T.reshape(-1)

    new_t = lambda name: pr[name][0].T
    zeros = jnp.zeros((db, NSA_KVW), F32)
    new_rows = jnp.stack([new_t("lkt"), new_t("lvt"), new_t("wkt"), new_t("wvt"), zeros, zeros, zeros, zeros], axis=1)
    ng = pr["ngt"][0].reshape(NSA_GROUPS, NG_ROWS, db)[:, :HEADS_PER_GROUP * N_NSA_BRANCH]
    ng = ng.reshape(NSA_HEADS, N_NSA_BRANCH, db).transpose(2, 0, 1)
    ng_s = jnp.pad(ng, ((0, 0), (0, 0), (0, LANES - N_NSA_BRANCH)))
    win_kt = _channel_major(win_k)
    win_vt = _channel_major(win_v)
    o_all = _nsa_sel_sample(qn_s, c_slc_k, c_slc_v, win_kt, win_vt, new_rows, ng_s, o_cmp, pt_flat, idx_flat, layer,
                            n_pool, n_pages)
    o5 = o_all.reshape(db, NSA_GROUPS, HEADS_PER_GROUP, NSA_GROUPS, NSA_DIM)
    o_nsa = jnp.stack([o5[:, g, :, g, :] for g in range(NSA_GROUPS)], axis=1).reshape(1, db, NSA_QW)

    y = _merge(x.reshape(1, db, d), o_sb.reshape(1, db, SB_W).astype(BF16), o_nsa.astype(BF16), pr["mg"],
               wl["w_up_sb"], wl["w_up_nsa"], wl["w_out"], wl["norm2_g"], wl["w_ff_in"], wl["w_ff_out"],
               norm_f_g, final_norm, db)
    n_buf = win_k.shape[1]
    n_keep = min(WINDOW, past + n_new)
    new_state = lambda name, groups: _token_major(pr[name][0][None], groups)[0][:, None]
    wk_all = jnp.concatenate([win_k, new_state("wkt", NSA_GROUPS)], axis=1)
    wv_all = jnp.concatenate([win_v, new_state("wvt", NSA_GROUPS)], axis=1)
    state = (new_state("skt", SB_HEADS), new_state("svt", SB_HEADS),
             new_state("ckt", NSA_GROUPS), new_state("cvt", NSA_GROUPS),
             new_state("lkt", NSA_GROUPS), new_state("lvt", NSA_GROUPS),
             wk_all[:, n_buf + n_new - n_keep:], wv_all[:, n_buf + n_new - n_keep:])
    return y.reshape(db, n_new, d), state


def kernel(x_prompt, x_sample, cache_sb_k, cache_sb_v, cache_cmp_k, cache_cmp_v, cache_slc_k, cache_slc_v,
           state_win_k, state_win_v, page_table, norm1_g, w_in, cmp_pool_k, cmp_pool_v, cmp_proj_k, cmp_proj_v,
           w_up_sb, w_up_nsa, w_out, norm2_g, w_ff_in, w_ff_out, norm_f_g):
    depth = w_in.shape[0]
    d_model = x_prompt.shape[-1]
    n_pool = cache_sb_k.shape[1]
    h_p, h_s = x_prompt, x_sample
    pages = lambda c: _channel_major(c).reshape((depth * n_pool,) + (c.shape[3] * c.shape[4], c.shape[2]))
    caches_t = tuple(pages(c) for c in (cache_sb_k, cache_sb_v, cache_cmp_k, cache_cmp_v, cache_slc_k, cache_slc_v))
    new_p, new_s = [], []
    for l in range(depth):
        w_main, w_t = _pack_proj_weights(w_in[l], d_model)
        wl = dict(norm1_g=norm1_g[l], w_main=w_main, w_t=w_t,
                  pool_k=cmp_pool_k[l].reshape(CMP_LEN, NSA_KVW), pool_v=cmp_pool_v[l].reshape(CMP_LEN, NSA_KVW),
                  wbd_k=_block_diag(cmp_proj_k[l]), wbd_v=_block_diag(cmp_proj_v[l]),
                  w_up_sb=w_up_sb[l].astype(BF16), w_up_nsa=w_up_nsa[l].astype(BF16), w_out=w_out[l].astype(BF16),
                  norm2_g=norm2_g[l], w_ff_in=w_ff_in[l].astype(BF16), w_ff_out=w_ff_out[l].astype(BF16))
        last = l == depth - 1
        h_p, st_p = _prompt_layer(h_p, wl, norm_f_g, last)
        h_s, st_s = _sample_layer(h_s, l, caches_t, n_pool, state_win_k[l], state_win_v[l], page_table, wl,
                                  norm_f_g, last)
        new_p.append(st_p)
        new_s.append(st_s)
    stk = lambda lst, j: jnp.stack([st[j] for st in lst])
    return (h_p, h_s) + tuple(stk(new_p, j) for j in range(8)) + tuple(stk(new_s, j) for j in range(8))
```

```python
import functools

import numpy as np
import jax
import jax.numpy as jnp
from jax import lax
from jax.experimental import pallas as pl
from jax.experimental.pallas import tpu as pltpu

SB_HEADS = 8
SB_DIM = 64
NSA_HEADS = 8
NSA_GROUPS = 2
NSA_DIM = 64
HEADS_PER_GROUP = NSA_HEADS // NSA_GROUPS
CMP_LEN = 32
CMP_STRIDE = 16
CMP_RATIO = CMP_LEN // CMP_STRIDE
SLC_BLOCK = 64
N_SEL = 16
WINDOW = 512
N_NSA_BRANCH = 3
EPS = 1e-6
NEG = -1e30
FORCE = 1e4
LOWEST = -3e38
LOG2E = 1.4426950408889634

LANES = 128
VMEM_LIMIT = 56 * 1024 * 1024

SB_W = SB_HEADS * SB_DIM
NSA_QW = NSA_HEADS * NSA_DIM
NSA_KVW = NSA_GROUPS * NSA_DIM

F32 = jnp.float32
BF16 = jnp.bfloat16
NT_DIMS = (((1,), (1,)), ((), ()))


def _dot(a, b):
    return jnp.dot(a, b, preferred_element_type=F32)


def _dot_nt(a, b):
    return lax.dot_general(a, b, NT_DIMS, preferred_element_type=F32)


def _split_bf16(x):
    hi = x.astype(BF16)
    lo = (x - hi.astype(F32)).astype(BF16)
    return hi, lo


def _params(*sem):
    return pltpu.CompilerParams(dimension_semantics=sem, vmem_limit_bytes=VMEM_LIMIT)


def _const_spec(shape):
    nd = len(shape)
    return pl.BlockSpec(shape, lambda *_: (0,) * nd, pipeline_mode=pl.Buffered(1))


def _channel_major(t):
    lead = t.shape[:-3]
    n = len(lead)
    t = jnp.transpose(t, tuple(range(n)) + (n + 1, n + 2, n))
    return t.reshape(lead + (t.shape[-3] * t.shape[-2], t.shape[-1]))


def _token_major(t, groups):
    lead = t.shape[:-2]
    n = len(lead)
    t = t.reshape(lead + (groups, t.shape[-2] // groups, t.shape[-1]))
    return jnp.transpose(t, tuple(range(n)) + (n + 2, n, n + 1))


_C_SQ = 0
_C_NQ = _C_SQ + SB_W
_C_CK = _C_NQ + NSA_HEADS * LANES
_C_CV = _C_CK + NSA_KVW
_C_LK = _C_CV + NSA_KVW
_C_WK = _C_LK + NSA_KVW
_C_MG = _C_WK + NSA_KVW
_R_SK = 0
_R_SV = _R_SK + SB_W
_R_CK = _R_SV + SB_W
_R_CV = _R_CK + NSA_KVW
_R_LK = _R_CV + NSA_KVW
_R_LV = _R_LK + NSA_KVW
_R_WK = _R_LV + NSA_KVW
_R_WV = _R_WK + NSA_KVW
_R_NG = _R_WV + NSA_KVW
NG_ROWS = 16
_R_END = _R_NG + NSA_GROUPS * NG_ROWS


def _proj_kernel(x_ref, g_ref, w_ref, wt_ref,
                 sq_o, nq_o, ck_o, cv_o, lkb_o, wkb_o, mg_o,
                 skt_o, sktb_o, svt_o, svtb_o, ckt_o, cvt_o, lkt_o, lvt_o, lvtb_o, wkt_o, wvt_o, wvtb_o, ngt_o,
                 *, d_mg):
    x = x_ref[0]
    ms = jnp.mean(x * x, axis=-1, keepdims=True)
    xn = (x * lax.rsqrt(ms + EPS) * g_ref[...]).astype(BF16)

    def cols(c0, width):
        return _dot(xn, w_ref[:, c0:c0 + width])

    all_rows = _dot_nt(wt_ref[...], xn)

    def rows(r0, height):
        return all_rows[r0:r0 + height]

    sq_o[0] = (cols(_C_SQ, SB_W) * (SB_DIM ** -0.5 * LOG2E)).astype(BF16)
    nq_o[0] = (cols(_C_NQ, NSA_HEADS * LANES) * (NSA_DIM ** -0.5 * LOG2E)).astype(BF16)
    ck_o[0] = cols(_C_CK, NSA_KVW)
    cv_o[0] = cols(_C_CV, NSA_KVW)
    lkb_o[0] = cols(_C_LK, NSA_KVW).astype(BF16)
    wkb_o[0] = cols(_C_WK, NSA_KVW).astype(BF16)
    mg_o[0] = cols(_C_MG, d_mg)
    skt = rows(_R_SK, SB_W)
    skt_o[0] = skt
    sktb_o[0] = skt.astype(BF16)
    svt = rows(_R_SV, SB_W)
    svt_o[0] = svt
    svtb_o[0] = svt.astype(BF16)
    ckt_o[0] = rows(_R_CK, NSA_KVW)
    cvt_o[0] = rows(_R_CV, NSA_KVW)
    lkt_o[0] = rows(_R_LK, NSA_KVW)
    lvt = rows(_R_LV, NSA_KVW)
    lvt_o[0] = lvt
    lvtb_o[0] = lvt.astype(BF16)
    wkt_o[0] = rows(_R_WK, NSA_KVW)
    wvt = rows(_R_WV, NSA_KVW)
    wvt_o[0] = wvt
    wvtb_o[0] = wvt.astype(BF16)
    ngt_o[0] = rows(_R_NG, NSA_GROUPS * NG_ROWS)


def _pack_proj_weights(w_in, d_model):
    splits = np.cumsum([SB_W, SB_W, SB_W, NSA_QW, NSA_KVW, NSA_KVW, NSA_KVW, NSA_KVW, NSA_KVW, NSA_KVW,
                        NSA_HEADS * N_NSA_BRANCH])
    sq, sk, sv, nq, ck, cv, lk, lv, wk, wv, ng, mg = jnp.split(w_in, splits.tolist(), axis=1)
    nq4 = nq.reshape(d_model, NSA_GROUPS, HEADS_PER_GROUP, NSA_DIM)
    nq_pad = jnp.zeros((d_model, NSA_GROUPS, HEADS_PER_GROUP, NSA_GROUPS, NSA_DIM), w_in.dtype)
    for g in range(NSA_GROUPS):
        nq_pad = nq_pad.at[:, g, :, g, :].set(nq4[:, g])
    nq_pad = nq_pad.reshape(d_model, NSA_HEADS * LANES)
    w_main = jnp.concatenate([sq, nq_pad, ck, cv, lk, wk, mg], axis=1).astype(BF16)
    ng3 = ng.reshape(d_model, NSA_GROUPS, HEADS_PER_GROUP * N_NSA_BRANCH)
    ng3 = jnp.pad(ng3, ((0, 0), (0, 0), (0, NG_ROWS - HEADS_PER_GROUP * N_NSA_BRANCH)))
    w_t = jnp.concatenate([sk, sv, ck, cv, lk, lv, wk, wv, ng3.reshape(d_model, NSA_GROUPS * NG_ROWS)],
                          axis=1).T.astype(BF16)
    return w_main, w_t


def _project(x, norm_g, w_main, w_t, tm):
    b, s, d = x.shape
    d_mg = w_main.shape[1] - _C_MG
    row = lambda width: pl.BlockSpec((1, tm, width), lambda bi, i: (bi, i, 0))
    col = lambda rows: pl.BlockSpec((1, rows, tm), lambda bi, i: (bi, 0, i))
    tok = lambda width, dt: jax.ShapeDtypeStruct((b, s, width), dt)
    chn = lambda rows, dt: jax.ShapeDtypeStruct((b, rows, s), dt)
    out_shape = (tok(SB_W, BF16), tok(NSA_HEADS * LANES, BF16), tok(NSA_KVW, F32), tok(NSA_KVW, F32),
                 tok(NSA_KVW, BF16), tok(NSA_KVW, BF16), tok(d_mg, F32),
                 chn(SB_W, F32), chn(SB_W, BF16), chn(SB_W, F32), chn(SB_W, BF16),
                 chn(NSA_KVW, F32), chn(NSA_KVW, F32), chn(NSA_KVW, F32), chn(NSA_KVW, F32), chn(NSA_KVW, BF16),
                 chn(NSA_KVW, F32), chn(NSA_KVW, F32), chn(NSA_KVW, BF16), chn(NSA_GROUPS * NG_ROWS, F32))
    out_specs = (row(SB_W), row(NSA_HEADS * LANES), row(NSA_KVW), row(NSA_KVW), row(NSA_KVW), row(NSA_KVW),
                 row(d_mg),
                 col(SB_W), col(SB_W), col(SB_W), col(SB_W),
                 col(NSA_KVW), col(NSA_KVW), col(NSA_KVW), col(NSA_KVW), col(NSA_KVW),
                 col(NSA_KVW), col(NSA_KVW), col(NSA_KVW), col(NSA_GROUPS * NG_ROWS))
    names = ("sq", "nq", "ck", "cv", "lkb", "wkb", "mg", "skt", "sktb", "svt", "svtb", "ckt", "cvt", "lkt", "lvt",
             "lvtb", "wkt", "wvt", "wvtb", "ngt")
    outs = pl.pallas_call(
        functools.partial(_proj_kernel, d_mg=d_mg),
        grid=(b, s // tm),
        in_specs=[row(d), _const_spec((1, d)), _const_spec(w_main.shape), _const_spec(w_t.shape)],
        out_specs=out_specs,
        out_shape=out_shape,
        compiler_params=_params("parallel", "parallel"),
        name="in_projection",
    )(x, norm_g.reshape(1, d), w_main, w_t)
    return dict(zip(names, outs))


def _compress_kernel(ck_ref, cv_ref, pk_ref, pv_ref, wk_ref, wvt_ref, kc_o, vct_o):
    n_sub = ck_ref.shape[1] // CMP_STRIDE

    def pooled(src, pw_ref):
        pw = pw_ref[...]
        a = jnp.zeros((n_sub, NSA_KVW), F32)
        b = jnp.zeros((n_sub, NSA_KVW), F32)
        for r in range(CMP_STRIDE):
            rows = src[0, pl.ds(r, n_sub, stride=CMP_STRIDE), :]
            a = a + rows * pw[r:r + 1, :]
            b = b + rows * pw[CMP_STRIDE + r:CMP_STRIDE + r + 1, :]
        return a + pltpu.roll(b, n_sub - 1, axis=0)

    kc_o[0] = _dot(pooled(ck_ref, pk_ref).astype(BF16), wk_ref[...]).astype(BF16)
    vct_o[0] = _dot_nt(wvt_ref[...], pooled(cv_ref, pv_ref).astype(BF16)).astype(BF16)


def _block_diag(proj):
    out = jnp.zeros((NSA_KVW, NSA_KVW), proj.dtype)
    for g in range(NSA_GROUPS):
        out = out.at[g * NSA_DIM:(g + 1) * NSA_DIM, g * NSA_DIM:(g + 1) * NSA_DIM].set(proj[g])
    return out


def _compress_prompt(ck, cv, pool_k, pool_v, wbd_k, wbd_v):
    b, s, _ = ck.shape
    n_sub = s // CMP_STRIDE
    kv = pl.BlockSpec((1, s, NSA_KVW), lambda bi: (bi, 0, 0))
    return pl.pallas_call(
        _compress_kernel,
        grid=(b,),
        in_specs=[kv, kv, _const_spec((CMP_LEN, NSA_KVW)), _const_spec((CMP_LEN, NSA_KVW)),
                  _const_spec((NSA_KVW, NSA_KVW)), _const_spec((NSA_KVW, NSA_KVW))],
        out_specs=(pl.BlockSpec((1, n_sub, NSA_KVW), lambda bi: (bi, 0, 0)),
                   pl.BlockSpec((1, NSA_KVW, n_sub), lambda bi: (bi, 0, 0))),
        out_shape=(jax.ShapeDtypeStruct((b, n_sub, NSA_KVW), BF16),
                   jax.ShapeDtypeStruct((b, NSA_KVW, n_sub), BF16)),
        compiler_params=_params("parallel"),
        name="compress_prompt",
    )(ck, cv, pool_k, pool_v, wbd_k.astype(BF16), wbd_v.T.astype(BF16))


def _softplus2(u):
    return jnp.maximum(u, 0.0) + jnp.log2(1.0 + jnp.exp2(-jnp.abs(u)))


def _strict_lower(n):
    return jnp.asarray(np.tril(np.ones((n, n), np.float32), -1), BF16)


SB_UNROLL = 4


def _sb_prompt_kernel(q_ref, kt_ref, vt_ref, tri_ref, o_ref, r_ref, acc_ref, *, t):
    i = pl.program_id(2)
    q = q_ref[0]
    lane = lax.broadcasted_iota(jnp.int32, (t, LANES), 1)
    zero = jnp.zeros_like(q)
    q2 = jnp.concatenate([jnp.where(lane < SB_DIM, q, zero), jnp.where(lane >= SB_DIM, q, zero)], axis=0)
    tri = tri_ref[...]
    key_i = lax.broadcasted_iota(jnp.int32, (2 * t, t), 1)
    row_i = lax.broadcasted_iota(jnp.int32, (2 * t, t), 0)
    causal = key_i < (row_i & (t - 1))

    def group(tile_ids, first_is_diag):
        parts = []
        for n, j in enumerate(tile_ids):
            off = pl.multiple_of(j * t, t)
            z = _dot(q2, kt_ref[0, :, pl.ds(off, t)])
            sp = _softplus2(z)
            diag = first_is_diag and n == 0
            drop = jnp.where(causal, sp, 0.0) if diag else sp
            local = _dot(drop.astype(BF16), tri)
            parts.append((off, diag, z - sp - local, jnp.sum(drop, axis=-1, keepdims=True)))
        r = r_ref[...]
        acc = acc_ref[...]
        for off, diag, arg, total in parts:
            a = jnp.exp2(arg + r)
            if diag:
                a = jnp.where(causal, a, 0.0)
            acc = acc + _dot_nt(a.astype(BF16), vt_ref[0, :, pl.ds(off, t)])
            r = r - total
        r_ref[...] = r
        acc_ref[...] = acc

    r_ref[...] = jnp.zeros_like(r_ref)
    acc_ref[...] = jnp.zeros_like(acc_ref)
    extra = i & (SB_UNROLL - 1)
    for n_extra in range(SB_UNROLL):
        @pl.when(extra == n_extra)
        def _(n_extra=n_extra):
            group(tuple(i - n for n in range(n_extra + 1)), True)

    top = i - 1 - extra

    def full_group(p, carry):
        group(tuple(top - SB_UNROLL * p - n for n in range(SB_UNROLL)), False)
        return carry

    lax.fori_loop(0, i // SB_UNROLL, full_group, 0)
    o_ref[0] = jnp.where(lane < SB_DIM, acc_ref[0:t, :], acc_ref[t:2 * t, :]).astype(o_ref.dtype)


def _sb_prompt(sq, skt, svt, t):
    b, s, w = sq.shape
    qo = pl.BlockSpec((1, t, LANES), lambda bi, hp, i: (bi, i, hp))
    kv = pl.BlockSpec((1, LANES, s), lambda bi, hp, i: (bi, hp, 0))
    return pl.pallas_call(
        functools.partial(_sb_prompt_kernel, t=t),
        grid=(b, w // LANES, s // t),
        in_specs=[qo, kv, kv, _const_spec((t, t))],
        out_specs=qo,
        out_shape=jax.ShapeDtypeStruct((b, s, w), BF16),
        scratch_shapes=[pltpu.VMEM((2 * t, 1), F32), pltpu.VMEM((2 * t, LANES), F32)],
        compiler_params=_params("parallel", "parallel", "arbitrary"),
        name="sb_prompt",
    )(sq, skt, svt, _strict_lower(t))


def _sb_sample_kernel(pt_ref, q_ref, *refs, n_pc):
    k_refs = refs[:n_pc]
    v_refs = refs[n_pc:2 * n_pc]
    tri_ref = refs[2 * n_pc]
    o_ref = refs[2 * n_pc + 1]
    r_ref, acc_ref = refs[2 * n_pc + 2:]
    c = pl.program_id(1)
    page = k_refs[0].shape[-1]

    @pl.when(c == 0)
    def _():
        r_ref[...] = jnp.zeros_like(r_ref)
        acc_ref[...] = jnp.zeros_like(acc_ref)

    q_col = q_ref[0].reshape(SB_HEADS, SB_DIM, page)
    scores = []
    for ii in range(n_pc):
        kt = k_refs[ii][0].reshape(SB_HEADS, SB_DIM, page)
        scores.append(jnp.sum(kt * q_col, axis=1))
    z = jnp.concatenate(scores, axis=0)
    sp = _softplus2(z)
    log_keep = -sp
    hi, lo = _split_bf16(log_keep)
    both = _dot(jnp.concatenate([hi, lo], axis=0), tri_ref[...])
    arg = z - sp + both[:n_pc * SB_HEADS] + both[n_pc * SB_HEADS:]
    totals = jnp.sum(log_keep, axis=-1, keepdims=True)
    r = r_ref[:, 0:1]
    acc = acc_ref[...]
    for ii in range(n_pc):
        rows = slice(ii * SB_HEADS, (ii + 1) * SB_HEADS)
        a = jnp.exp2(arg[rows] + r)
        vt = v_refs[ii][0].reshape(SB_HEADS, SB_DIM, page)
        acc = acc + vt * a[:, None, :]
        r = r + totals[rows]
    r_ref[...] = jnp.broadcast_to(r, r_ref.shape)
    acc_ref[...] = acc

    @pl.when(c == pl.num_programs(1) - 1)
    def _():
        o_ref[0] = jnp.sum(acc, axis=-1, keepdims=True).reshape(SB_W, 1)


def _sb_sample(sq_s, cache_kt, cache_vt, pt_flat, layer, n_pool, n_pages, n_pc):
    db = sq_s.shape[0]
    page = cache_kt.shape[-1]
    n_chunks = n_pages // n_pc

    def page_spec(ii):
        def imap(bi, c, pt):
            p = n_pages - 1 - (c * n_pc + ii)
            return (layer * n_pool + jnp.clip(pt[bi * n_pages + p], 0, n_pool - 1), 0, 0)
        return pl.BlockSpec((1, SB_W, page), imap)

    specs = [pl.BlockSpec((1, SB_W, page), lambda bi, c, pt: (bi, 0, 0))]
    specs += [page_spec(ii) for ii in range(n_pc)] * 2
    specs += [pl.BlockSpec((page, page), lambda bi, c, pt: (0, 0))]
    grid_spec = pltpu.PrefetchScalarGridSpec(
        num_scalar_prefetch=1, grid=(db, n_chunks), in_specs=specs,
        out_specs=pl.BlockSpec((1, SB_W, 1), lambda bi, c, pt: (bi, 0, 0)),
        scratch_shapes=[pltpu.VMEM((SB_HEADS, LANES), F32), pltpu.VMEM((SB_HEADS, SB_DIM, page), F32)])
    q_lanes = jnp.broadcast_to(sq_s.astype(F32)[:, :, None], (db, SB_W, page))
    out = pl.pallas_call(
        functools.partial(_sb_sample_kernel, n_pc=n_pc),
        grid_spec=grid_spec,
        out_shape=jax.ShapeDtypeStruct((db, SB_W, 1), F32),
        compiler_params=_params("parallel", "arbitrary"),
        name="sb_sample",
    )(pt_flat, q_lanes, *([cache_kt] * n_pc), *([cache_vt] * n_pc), _strict_lower(page))
    return out.reshape(db, SB_W)


def _rank_desc(v):
    n = v.shape[0]
    sub = lax.broadcasted_iota(jnp.int32, (8, v.shape[1]), 0)
    cnt = jnp.zeros(v.shape, jnp.int32)
    for m in range(n):
        rowv = v[m:m + 1, :]
        g0 = (m // 8) * 8
        parts = []
        if g0 > 0:
            parts.append(jnp.where(rowv > v[:g0], 1, 0))
        blk = v[g0:g0 + 8]
        parts.append(jnp.where(sub > (m % 8), jnp.where(rowv >= blk, 1, 0), jnp.where(rowv > blk, 1, 0)))
        if g0 + 8 < n:
            parts.append(jnp.where(rowv >= v[g0 + 8:], 1, 0))
        cnt = cnt + (parts[0] if len(parts) == 1 else jnp.concatenate(parts, axis=0))
    return cnt


def _cmp_to_slc_t(n_cmp, n_slc, n_cmp_pad, n_slc_pad):
    c_start = np.arange(n_cmp) * CMP_STRIDE
    s_start = np.arange(n_slc) * SLC_BLOCK
    inter = (np.minimum(c_start[None, :] + CMP_LEN, s_start[:, None] + SLC_BLOCK)
             - np.maximum(c_start[None, :], s_start[:, None]))
    ov = np.zeros((n_slc_pad, n_cmp_pad), np.float32)
    ov[:n_slc, :n_cmp] = np.clip(inter, 0, None) / CMP_LEN
    return ov


def _alibi_slopes():
    return (2.0 ** (-8.0 * (np.arange(NSA_HEADS) + 1) / NSA_HEADS) * LOG2E).astype(np.float32)


def _nsa_prompt_kernel(q_ref, lk_ref, lvt_ref, wk_ref, wvt_ref, kc_ref, vct_ref, ngt_ref, slope_ref, ovt_ref,
                       o_ref, sel_ref, bias_ref, *, tq, tks, tkw, n_sel):
    i = pl.program_id(2)
    nh = HEADS_PER_GROUP
    w = nh * tq
    q0 = i * tq
    qs = jnp.concatenate([q_ref[0, :, h * LANES:(h + 1) * LANES] for h in range(nh)], axis=0)
    lane = lax.broadcasted_iota(jnp.int32, (1, w), 1)
    qpos = q0 + (lane & (tq - 1))
    slope = slope_ref[0]

    kc = kc_ref[0]
    n_c = kc.shape[0]
    c_end = lax.broadcasted_iota(jnp.int32, (n_c, 1), 0) * CMP_STRIDE + (CMP_LEN - 1)
    dist = qpos - c_end
    valid = dist >= 0
    s = _dot_nt(kc, qs) - slope * dist.astype(F32)
    sm = jnp.where(valid, s, NEG)
    m = jnp.max(sm, axis=0, keepdims=True)
    e = jnp.where(valid, jnp.exp2(sm - m), 0.0)
    l = jnp.sum(e, axis=0, keepdims=True)
    p = e * (1.0 / jnp.where(l > 0.0, l, 1.0))
    o_cmp = _dot(vct_ref[0], p.astype(BF16))

    p_sum = p[:, 0:tq]
    for h in range(1, nh):
        p_sum = p_sum + p[:, h * tq:(h + 1) * tq]
    hi, lo = _split_bf16(p_sum)
    ovt = ovt_ref[...]
    imp = _dot(ovt, hi) + _dot(ovt, lo)
    n_slc = imp.shape[0]
    blk = lax.broadcasted_iota(jnp.int32, (n_slc, tq), 0)
    cur = (q0 + lax.broadcasted_iota(jnp.int32, (1, tq), 1)) // SLC_BLOCK
    forced = (blk == 0) | (blk == cur) | (blk == cur - 1)
    imp = jnp.where(forced, FORCE, jnp.where(blk <= cur, imp, -FORCE))
    sel_ref[...] = jnp.where(_rank_desc(imp) < n_sel, 0.0, NEG)

    bias_ref[...] = slope * lax.broadcasted_iota(jnp.int32, (tks, 1), 0).astype(F32)

    def sel_tile(j, carry, causal):
        m_old, l_old, acc = carry
        off = pl.multiple_of(j * tks, tks)
        rows = []
        for bb in range(tks // SLC_BLOCK):
            r = sel_ref[pl.ds(j * (tks // SLC_BLOCK) + bb, 1), :]
            rows.append(jnp.broadcast_to(r, (SLC_BLOCK, tq)))
        picked = jnp.concatenate(rows, axis=0)
        picked = jnp.concatenate([picked] * nh, axis=1)
        sm = _dot_nt(lk_ref[0, pl.ds(off, tks), :], qs) + bias_ref[...] + picked
        if causal:
            t_pos = off + lax.broadcasted_iota(jnp.int32, (tks, 1), 0)
            sm = jnp.where(qpos >= t_pos, sm, NEG)
        shift = slope * off.astype(F32)
        m_new = jnp.maximum(m_old, jnp.max(sm, axis=0, keepdims=True) + shift)
        alpha = jnp.exp2(m_old - m_new)
        p_t = jnp.exp2(sm - (m_new - shift))
        l_new = alpha * l_old + jnp.sum(p_t, axis=0, keepdims=True)
        acc = alpha * acc + _dot(lvt_ref[0, :, pl.ds(off, tks)], p_t.astype(BF16))
        return m_new, l_new, acc

    init = (jnp.full((1, w), NEG, F32), jnp.zeros((1, w), F32), jnp.zeros((NSA_DIM, w), F32))
    n_full = q0 // tks
    carry = lax.fori_loop(0, n_full, lambda j, c: sel_tile(j, c, False), init)
    _, l_sel, acc_sel = sel_tile(n_full, carry, True)

    win_scores = []
    for d in range(WINDOW // tkw + 1):
        jt = i - d
        off = pl.multiple_of(jnp.maximum(jt, 0) * tkw, tkw)
        limit = jnp.where(jt >= 0, WINDOW, -1)
        dist_t = qpos - (off + lax.broadcasted_iota(jnp.int32, (tkw, 1), 0))
        valid_t = jnp.where(dist_t <= limit, dist_t, -1) >= 0
        s_t = _dot_nt(wk_ref[0, pl.ds(off, tkw), :], qs) - slope * dist_t.astype(F32)
        win_scores.append((off, jnp.where(valid_t, s_t, NEG)))
    m_win = jnp.max(win_scores[0][1], axis=0, keepdims=True)
    for _, sm_t in win_scores[1:]:
        m_win = jnp.maximum(m_win, jnp.max(sm_t, axis=0, keepdims=True))
    l_win = jnp.zeros((1, w), F32)
    acc_win = jnp.zeros((NSA_DIM, w), F32)
    for off, sm_t in win_scores:
        p_t = jnp.exp2(sm_t - m_win)
        l_win = l_win + jnp.sum(p_t, axis=0, keepdims=True)
        acc_win = acc_win + _dot(wvt_ref[0, :, pl.ds(off, tkw)], p_t.astype(BF16))

    gates = jax.nn.sigmoid(ngt_ref[0])
    o_sel = acc_sel * (1.0 / l_sel)
    o_win = acc_win * (1.0 / l_win)
    heads = []
    for h in range(nh):
        hs = slice(h * tq, (h + 1) * tq)
        g0 = gates[h * N_NSA_BRANCH + 0:h * N_NSA_BRANCH + 1, :]
        g1 = gates[h * N_NSA_BRANCH + 1:h * N_NSA_BRANCH + 2, :]
        g2 = gates[h * N_NSA_BRANCH + 2:h * N_NSA_BRANCH + 3, :]
        heads.append(g0 * o_cmp[:, hs] + g1 * o_sel[:, hs] + g2 * o_win[:, hs])
    for hp in range(nh // 2):
        pair = jnp.concatenate([heads[2 * hp], heads[2 * hp + 1]], axis=0)
        o_ref[0, :, hp * LANES:(hp + 1) * LANES] = pair.T.astype(o_ref.dtype)


def _nsa_prompt(nq_pad, lk_b, lvt, wk_b, wvt, kc, vct, ngt, tq, tks):
    b, s, _ = nq_pad.shape
    assert tq == LANES and s % tks == 0 and tks % SLC_BLOCK == 0
    n_c = kc.shape[1]
    n_slc = s // SLC_BLOCK
    n_cmp = n_c - CMP_RATIO + 1
    w = HEADS_PER_GROUP * tq
    slopes = np.repeat(_alibi_slopes().reshape(NSA_GROUPS, HEADS_PER_GROUP), tq, axis=1).reshape(NSA_GROUPS, 1, w)
    ovt = jnp.asarray(_cmp_to_slc_t(n_cmp, n_slc, n_c, n_slc), BF16)
    full = lambda rows, lanes: pl.BlockSpec((1, rows, lanes), lambda bi, g, i: (bi, 0, 0))
    grp = lambda rows, lanes: pl.BlockSpec((1, rows, lanes), lambda bi, g, i: (bi, g, 0))
    return pl.pallas_call(
        functools.partial(_nsa_prompt_kernel, tq=tq, tks=tks, tkw=tq, n_sel=min(N_SEL, n_slc)),
        grid=(b, NSA_GROUPS, s // tq),
        in_specs=[pl.BlockSpec((1, tq, HEADS_PER_GROUP * LANES), lambda bi, g, i: (bi, i, g)),
                  full(s, NSA_KVW), grp(NSA_DIM, s), full(s, NSA_KVW), grp(NSA_DIM, s),
                  full(n_c, NSA_KVW), grp(NSA_DIM, n_c),
                  pl.BlockSpec((1, NG_ROWS, tq), lambda bi, g, i: (bi, g, i)),
                  pl.BlockSpec((1, 1, w), lambda bi, g, i: (g, 0, 0)),
                  _const_spec((n_slc, n_c))],
        out_specs=pl.BlockSpec((1, tq, HEADS_PER_GROUP * NSA_DIM), lambda bi, g, i: (bi, i, g)),
        out_shape=jax.ShapeDtypeStruct((b, s, NSA_QW), BF16),
        scratch_shapes=[pltpu.VMEM((n_slc, tq), F32), pltpu.VMEM((tks, w), F32)],
        compiler_params=_params("parallel", "parallel", "arbitrary"),
        name="nsa_prompt",
    )(nq_pad, lk_b, lvt, wk_b, wvt, kc, vct, ngt, jnp.asarray(slopes), ovt)


def _nsa_cmp_sample_kernel(pt_ref, q_ref, *refs, n_pc, past):
    k_refs = refs[:n_pc]
    v_refs = refs[n_pc:2 * n_pc]
    pk0_ref, pk1_ref, pv0_ref, pv1_ref, seg_ref, wkt_ref, wvt_ref, slope_ref, ov_ref = refs[2 * n_pc:2 * n_pc + 9]
    oc_o, imp_o = refs[2 * n_pc + 9:2 * n_pc + 11]
    ak_ref, bk_ref, av_ref, bv_ref = refs[2 * n_pc + 11:]
    c = pl.program_id(1)
    page = k_refs[0].shape[-1]
    m_chunk = n_pc * (page // CMP_STRIDE)

    off = pl.multiple_of(c * m_chunk, m_chunk)
    for page_refs, w0_ref, w1_ref, a_ref, b_ref in ((k_refs, pk0_ref, pk1_ref, ak_ref, bk_ref),
                                                    (v_refs, pv0_ref, pv1_ref, av_ref, bv_ref)):
        w0 = w0_ref[...]
        w1 = w1_ref[...]
        pooled = jnp.zeros((2 * NSA_KVW, m_chunk), F32)
        for pair in range(n_pc // 2):
            xa = page_refs[2 * pair][0]
            xb = page_refs[2 * pair + 1][0]
            lhs = jnp.concatenate([jnp.concatenate([xa * w0, xb * w0], axis=1),
                                   jnp.concatenate([xa * w1, xb * w1], axis=1)], axis=0).astype(BF16)
            pooled = pooled + _dot(lhs, seg_ref[pair])
        a_ref[:, pl.ds(off, m_chunk)] = pooled[:NSA_KVW]
        b_ref[:, pl.ds(off, m_chunk)] = pooled[NSA_KVW:]

    @pl.when(c == pl.num_programs(1) - 1)
    def _():
        n_sub = ak_ref.shape[1]
        pooled_k = ak_ref[...] + pltpu.roll(bk_ref[...], n_sub - 1, axis=1)
        pooled_v = av_ref[...] + pltpu.roll(bv_ref[...], n_sub - 1, axis=1)
        kct = _dot(wkt_ref[...], pooled_k.astype(BF16)).astype(BF16)
        vct = _dot(wvt_ref[...], pooled_v.astype(BF16)).astype(BF16)
        qn = q_ref[0].astype(BF16)
        c_end = lax.broadcasted_iota(jnp.int32, (1, n_sub), 1) * CMP_STRIDE + (CMP_LEN - 1)
        dist = past - c_end
        valid = dist >= 0
        s = _dot(qn, kct) - slope_ref[:, 0:1] * dist.astype(F32)
        sm = jnp.where(valid, s, NEG)
        m = jnp.max(sm, axis=-1, keepdims=True)
        e = jnp.where(valid, jnp.exp2(sm - m), 0.0)
        l = jnp.sum(e, axis=-1, keepdims=True)
        p = e * (1.0 / jnp.where(l > 0.0, l, 1.0))
        oc_o[0] = _dot_nt(p.astype(BF16), vct)
        sums = [jnp.sum(p[g * HEADS_PER_GROUP:(g + 1) * HEADS_PER_GROUP], axis=0, keepdims=True)
                for g in range(NSA_GROUPS)]
        sums.append(jnp.zeros((NSA_HEADS - NSA_GROUPS, n_sub), F32))
        hi, lo = _split_bf16(jnp.concatenate(sums, axis=0))
        ov = ov_ref[...]
        imp_o[0] = _dot(hi, ov) + _dot(lo, ov)


def _nsa_cmp_sample(qn_s, cache_kt, cache_vt, pt_flat, layer, n_pool, n_pages, n_pc, pool_k, pool_v, wbd_k, wbd_v,
                    n_slc_pad):
    db = qn_s.shape[0]
    page = cache_kt.shape[-1]
    past = n_pages * page
    n_sub = past // CMP_STRIDE
    n_cmp = n_sub - CMP_RATIO + 1
    n_slc = -(-(past + 1) // SLC_BLOCK)
    sub_per_page = page // CMP_STRIDE
    m_chunk = n_pc * sub_per_page
    assert n_pc % 2 == 0 and m_chunk % LANES == 0
    ov = jnp.asarray(_cmp_to_slc_t(n_cmp, n_slc, n_sub, n_slc_pad).T, BF16)
    slopes = jnp.asarray(np.repeat(_alibi_slopes()[:, None], LANES, axis=1))
    t_idx = np.arange(2 * page)
    seg = np.zeros((n_pc // 2, 2 * page, m_chunk), np.float32)
    for pair in range(n_pc // 2):
        seg[pair, t_idx, 2 * sub_per_page * pair + t_idx // CMP_STRIDE] = 1.0
    tile_w = lambda pw, half: jnp.tile(pw[half * CMP_STRIDE:(half + 1) * CMP_STRIDE].T, (1, sub_per_page))

    def page_spec(ii):
        return pl.BlockSpec((1, NSA_KVW, page),
                            lambda bi, c, pt: (layer * n_pool
                                               + jnp.clip(pt[bi * n_pages + c * n_pc + ii], 0, n_pool - 1), 0, 0))

    cst = lambda shape: pl.BlockSpec(shape, lambda bi, c, pt: (0,) * len(shape))
    specs = [pl.BlockSpec((1, NSA_HEADS, LANES), lambda bi, c, pt: (bi, 0, 0))]
    specs += [page_spec(ii) for ii in range(n_pc)] * 2
    specs += [cst((NSA_KVW, page))] * 4
    specs += [cst(seg.shape), cst((NSA_KVW, NSA_KVW)), cst((NSA_KVW, NSA_KVW)),
              cst((NSA_HEADS, LANES)), cst((n_sub, n_slc_pad))]
    out_blk = lambda lanes: pl.BlockSpec((1, NSA_HEADS, lanes), lambda bi, c, pt: (bi, 0, 0))
    grid_spec = pltpu.PrefetchScalarGridSpec(
        num_scalar_prefetch=1, grid=(db, n_pages // n_pc), in_specs=specs,
        out_specs=(out_blk(LANES), out_blk(n_slc_pad)),
        scratch_shapes=[pltpu.VMEM((NSA_KVW, n_sub), F32)] * 4)
    return pl.pallas_call(
        functools.partial(_nsa_cmp_sample_kernel, n_pc=n_pc, past=past),
        grid_spec=grid_spec,
        out_shape=(jax.ShapeDtypeStruct((db, NSA_HEADS, LANES), F32),
                   jax.ShapeDtypeStruct((db, NSA_HEADS, n_slc_pad), F32)),
        compiler_params=_params("parallel", "arbitrary"),
        name="nsa_cmp_sample",
    )(pt_flat, qn_s, *([cache_kt] * n_pc), *([cache_vt] * n_pc),
      tile_w(pool_k, 0), tile_w(pool_k, 1), tile_w(pool_v, 0), tile_w(pool_v, 1), jnp.asarray(seg, BF16),
      wbd_k.T.astype(BF16), wbd_v.T.astype(BF16), slopes, ov)


def _topk_sample_kernel(imp_ref, idx_o, *, n_slc, cur, n_sel):
    imp = imp_ref[...]
    blk = lax.broadcasted_iota(jnp.int32, imp.shape, 0)
    forced = (blk == 0) | (blk == cur) | (blk == cur - 1)
    imp = jnp.where(forced, FORCE, jnp.where(blk <= cur, imp, -FORCE))
    imp = jnp.where(blk < n_slc, imp, LOWEST)
    rank = _rank_desc(imp)
    rows = [jnp.sum(jnp.where(rank == r, blk, 0), axis=0, keepdims=True) for r in range(n_sel)]
    idx_o[...] = jnp.concatenate(rows, axis=0)


def _topk_sample(imp_t, n_slc, cur, n_sel):
    n_pad, cols = imp_t.shape
    return pl.pallas_call(
        functools.partial(_topk_sample_kernel, n_slc=n_slc, cur=cur, n_sel=n_sel),
        grid=(1,),
        in_specs=[_const_spec((n_pad, cols))],
        out_specs=_const_spec((n_sel, cols)),
        out_shape=jax.ShapeDtypeStruct((n_sel, cols), jnp.int32),
        compiler_params=_params("arbitrary"),
        name="topk_sample",
    )(imp_t)


def _nsa_sel_sample_kernel(pt_ref, idx_ref, q_ref, *refs, n_sel, n_cached, past, n_buf):
    n_blk = NSA_GROUPS * n_sel
    k_refs = refs[:n_blk]
    v_refs = refs[n_blk:2 * n_blk]
    wink_ref, winv_ref, new_ref, ng_ref, oc_ref, slope_ref, o_ref = refs[2 * n_blk:]
    bi = pl.program_id(0)
    page = k_refs[0].shape[-1]
    halves = page // SLC_BLOCK
    qn = q_ref[0]
    qb = qn.astype(BF16)
    rowi = lax.broadcasted_iota(jnp.int32, (NSA_HEADS, 1), 0)
    in_g0 = rowi < HEADS_PER_GROUP
    slope = slope_ref[:, 0:1]
    lk_new, lv_new, wk_new, wv_new = (new_ref[0, r:r + 1, :] for r in range(4))

    def new_score(k_row):
        return jnp.sum(qn * k_row, axis=-1, keepdims=True)

    width = n_sel * page
    lane = lax.broadcasted_iota(jnp.int32, (1, width), 1)
    slot = lane // page
    scores, values, blk_rows, has_new = [], [], [], []
    for g in range(NSA_GROUPS):
        ids = [idx_ref[(bi * NSA_GROUPS + g) * n_sel + r] for r in range(n_sel)]
        scores.append(jnp.concatenate(
            [_dot(qb, k_refs[g * n_sel + r][0].astype(BF16)) for r in range(n_sel)], axis=1))
        values.append(jnp.concatenate([v_refs[g * n_sel + r][0] for r in range(n_sel)], axis=1).astype(BF16))
        row = jnp.zeros((1, width), jnp.int32)
        flag = jnp.int32(0)
        for r in range(n_sel):
            row = jnp.where(slot == r, ids[r], row)
            flag = jnp.maximum(flag, jnp.where(ids[r] >= n_cached, 1, 0))
        blk_rows.append(row)
        has_new.append(flag)
    s = jnp.where(in_g0, scores[0], scores[1])
    blk = jnp.where(in_g0, blk_rows[0], blk_rows[1])
    page_blk = jnp.minimum(blk, n_cached - 1)
    tok = (page_blk // halves) * page + (lane & (page - 1))
    dist = past - tok
    valid = jnp.where(blk < n_cached, tok // SLC_BLOCK, -1) == blk
    s = s - slope * dist.astype(F32)
    sm = jnp.where(valid, s, NEG)
    new_on = jnp.where(in_g0, has_new[0], has_new[1]) > 0
    s_new = jnp.where(new_on, new_score(lk_new), NEG)
    m = jnp.maximum(jnp.max(sm, axis=-1, keepdims=True), s_new)
    e = jnp.where(valid, jnp.exp2(sm - m), 0.0)
    e_new = jnp.where(new_on, jnp.exp2(s_new - m), 0.0)
    l = jnp.sum(e, axis=-1, keepdims=True) + e_new
    eb = e.astype(BF16)
    o_sel = jnp.where(in_g0, _dot_nt(eb, values[0]), _dot_nt(eb, values[1])) + e_new * lv_new
    o_sel = o_sel * (1.0 / l)

    pos = lax.broadcasted_iota(jnp.int32, (1, n_buf), 1)
    dist_w = n_buf - pos
    valid_w = jnp.where(dist_w <= WINDOW, past - dist_w, -1) >= 0
    s_w = _dot(qb, wink_ref[0].astype(BF16)) - slope * dist_w.astype(F32)
    sm_w = jnp.where(valid_w, s_w, NEG)
    s_wn = new_score(wk_new)
    m_w = jnp.maximum(jnp.max(sm_w, axis=-1, keepdims=True), s_wn)
    e_w = jnp.where(valid_w, jnp.exp2(sm_w - m_w), 0.0)
    e_wn = jnp.exp2(s_wn - m_w)
    l_w = jnp.sum(e_w, axis=-1, keepdims=True) + e_wn
    o_win = (_dot_nt(e_w.astype(BF16), winv_ref[0].astype(BF16)) + e_wn * wv_new) * (1.0 / l_w)

    gates = jax.nn.sigmoid(ng_ref[0])
    o = gates[:, 0:1] * oc_ref[0] + gates[:, 1:2] * o_sel + gates[:, 2:3] * o_win
    lane_g = lax.broadcasted_iota(jnp.int32, (NSA_HEADS, LANES), 1) // NSA_DIM
    o_ref[0] = jnp.where(lane_g == rowi // HEADS_PER_GROUP, o, 0.0)


def _nsa_sel_sample(qn_s, cache_kt, cache_vt, win_kt, win_vt, new_rows, ng_s, o_cmp, pt_flat, idx_flat, layer,
                    n_pool, n_pages):
    db = qn_s.shape[0]
    page = cache_kt.shape[-1]
    past = n_pages * page
    halves = page // SLC_BLOCK
    n_cached = past // SLC_BLOCK
    n_sel = idx_flat.shape[0] // (db * NSA_GROUPS)
    n_buf = win_kt.shape[-1]
    slopes = jnp.asarray(np.repeat(_alibi_slopes()[:, None], LANES, axis=1))

    def blk_spec(g, r):
        def imap(bi, pt, idx):
            n = jnp.clip(idx[(bi * NSA_GROUPS + g) * n_sel + r], 0, n_cached - 1)
            return (layer * n_pool + jnp.clip(pt[bi * n_pages + n // halves], 0, n_pool - 1), 0, 0)
        return pl.BlockSpec((1, NSA_KVW, page), imap)

    per_req = lambda rows, lanes: pl.BlockSpec((1, rows, lanes), lambda bi, pt, idx: (bi, 0, 0))
    blocks = [blk_spec(g, r) for g in range(NSA_GROUPS) for r in range(n_sel)]
    specs = [per_req(NSA_HEADS, LANES)] + blocks + blocks
    specs += [per_req(NSA_KVW, n_buf), per_req(NSA_KVW, n_buf), per_req(8, NSA_KVW),
              per_req(NSA_HEADS, LANES), per_req(NSA_HEADS, LANES),
              pl.BlockSpec((NSA_HEADS, LANES), lambda bi, pt, idx: (0, 0))]
    grid_spec = pltpu.PrefetchScalarGridSpec(
        num_scalar_prefetch=2, grid=(db,), in_specs=specs, out_specs=per_req(NSA_HEADS, LANES))
    n_blk = len(blocks)
    return pl.pallas_call(
        functools.partial(_nsa_sel_sample_kernel, n_sel=n_sel, n_cached=n_cached, past=past, n_buf=n_buf),
        grid_spec=grid_spec,
        out_shape=jax.ShapeDtypeStruct((db, NSA_HEADS, LANES), F32),
        compiler_params=_params("parallel"),
        name="nsa_sel_sample",
    )(pt_flat, idx_flat, qn_s, *([cache_kt] * n_blk), *([cache_vt] * n_blk), win_kt, win_vt, new_rows, ng_s, o_cmp,
      slopes)


def _merge_kernel(x_ref, osb_ref, onsa_ref, mg_ref, wsb_ref, wnsa_ref, wout_ref, g2_ref, wfi_ref, wfo_ref, gf_ref,
                  y_ref, *, final_norm):
    d = x_ref.shape[-1]
    gate = jax.nn.sigmoid(mg_ref[0])
    mixed = gate[:, 0:d] * _dot(osb_ref[0], wsb_ref[...]) + gate[:, d:2 * d] * _dot(onsa_ref[0], wnsa_ref[...])
    x1 = x_ref[0] + _dot(mixed.astype(BF16), wout_ref[...])
    ms = jnp.mean(x1 * x1, axis=-1, keepdims=True)
    hn = (x1 * lax.rsqrt(ms + EPS) * g2_ref[...]).astype(BF16)
    u = jnp.maximum(_dot(hn, wfi_ref[...]), 0.0)
    x2 = x1 + _dot((u * u).astype(BF16), wfo_ref[...])
    if final_norm:
        ms2 = jnp.mean(x2 * x2, axis=-1, keepdims=True)
        x2 = x2 * lax.rsqrt(ms2 + EPS) * gf_ref[...]
    y_ref[0] = x2


def _merge(x, o_sb, o_nsa, mg, w_up_sb, w_up_nsa, w_out, norm2_g, w_ff_in, w_ff_out, norm_f_g, final_norm, tm):
    b, s, d = x.shape
    d_ff = w_ff_in.shape[1]
    row = lambda width: pl.BlockSpec((1, tm, width), lambda bi, i: (bi, i, 0))
    return pl.pallas_call(
        functools.partial(_merge_kernel, final_norm=final_norm),
        grid=(b, s // tm),
        in_specs=[row(d), row(SB_W), row(NSA_QW), row(2 * d),
                  _const_spec((SB_W, d)), _const_spec((NSA_QW, d)), _const_spec((d, d)), _const_spec((1, d)),
                  _const_spec((d, d_ff)), _const_spec((d_ff, d)), _const_spec((1, d))],
        out_specs=row(d),
        out_shape=jax.ShapeDtypeStruct((b, s, d), F32),
        compiler_params=_params("parallel", "parallel"),
        name="merge_mlp",
    )(x, o_sb, o_nsa, mg, w_up_sb, w_up_nsa, w_out, norm2_g.reshape(1, d), w_ff_in, w_ff_out, norm_f_g.reshape(1, d))


def _row_tile(n, target):
    t = min(n, target)
    assert n % t == 0
    return t


def _prompt_layer(x, wl, norm_f_g, final_norm):
    b, s, d = x.shape
    pr = _project(x, wl["norm1_g"], wl["w_main"], wl["w_t"], _row_tile(s, 256))
    kc, vct = _compress_prompt(pr["ck"], pr["cv"], wl["pool_k"], wl["pool_v"], wl["wbd_k"], wl["wbd_v"])
    o_sb = _sb_prompt(pr["sq"], pr["sktb"], pr["svtb"], _row_tile(s, 256))
    o_nsa = _nsa_prompt(pr["nq"], pr["lkb"], pr["lvtb"], pr["wkb"], pr["wvtb"], kc, vct, pr["ngt"], LANES, 1024)
    y = _merge(x, o_sb, o_nsa, pr["mg"], wl["w_up_sb"], wl["w_up_nsa"], wl["w_out"], wl["norm2_g"],
               wl["w_ff_in"], wl["w_ff_out"], norm_f_g, final_norm, _row_tile(s, 256))
    n_keep = min(WINDOW, s)
    state = (_token_major(pr["skt"], SB_HEADS), _token_major(pr["svt"], SB_HEADS),
             _token_major(pr["ckt"], NSA_GROUPS), _token_major(pr["cvt"], NSA_GROUPS),
             _token_major(pr["lkt"], NSA_GROUPS), _token_major(pr["lvt"], NSA_GROUPS),
             _token_major(pr["wkt"][:, :, s - n_keep:], NSA_GROUPS),
             _token_major(pr["wvt"][:, :, s - n_keep:], NSA_GROUPS))
    return y, state


def _sample_layer(x, layer, caches_t, n_pool, win_k, win_v, page_table, wl, norm_f_g, final_norm):
    c_sb_k, c_sb_v, c_cmp_k, c_cmp_v, c_slc_k, c_slc_v = caches_t
    db, n_new, d = x.shape
    assert n_new == 1
    n_pages = page_table.shape[1]
    page = c_sb_k.shape[-1]
    past = n_pages * page
    pt_flat = page_table.reshape(-1).astype(jnp.int32)
    pr = _project(x.reshape(1, db, d), wl["norm1_g"], wl["w_main"], wl["w_t"], db)
    n_pc = 16 if n_pages % 16 == 0 else 2
    o_sb = _sb_sample(pr["sq"][0], c_sb_k, c_sb_v, pt_flat, layer, n_pool, n_pages, n_pc)

    qn_s = pr["nq"][0].astype(F32).reshape(db, NSA_HEADS, LANES)
    n_slc = -(-(past + n_new) // SLC_BLOCK)
    cur = past // SLC_BLOCK
    n_slc_pad = -(-n_slc // LANES) * LANES
    o_cmp, imp = _nsa_cmp_sample(qn_s, c_cmp_k, c_cmp_v, pt_flat, layer, n_pool, n_pages, n_pc,
                                 wl["pool_k"], wl["pool_v"], wl["wbd_k"], wl["wbd_v"], n_slc_pad)
    n_rank = -(-n_slc // 8) * 8
    imp_t = imp[:, :NSA_GROUPS, :n_rank].reshape(db * NSA_GROUPS, n_rank).T
    n_sel = min(N_SEL, n_slc)
    idx = _topk_sample(imp_t, n_slc, cur, n_sel)
    idx_flat = idx.T.reshape(-1)

    new_t = lambda name: pr[name][0].T
    zeros = jnp.zeros((db, NSA_KVW), F32)
    new_rows = jnp.stack([new_t("lkt"), new_t("lvt"), new_t("wkt"), new_t("wvt"), zeros, zeros, zeros, zeros], axis=1)
    ng = pr["ngt"][0].reshape(NSA_GROUPS, NG_ROWS, db)[:, :HEADS_PER_GROUP * N_NSA_BRANCH]
    ng = ng.reshape(NSA_HEADS, N_NSA_BRANCH, db).transpose(2, 0, 1)
    ng_s = jnp.pad(ng, ((0, 0), (0, 0), (0, LANES - N_NSA_BRANCH)))
    win_kt = _channel_major(win_k)
    win_vt = _channel_major(win_v)
    o_all = _nsa_sel_sample(qn_s, c_slc_k, c_slc_v, win_kt, win_vt, new_rows, ng_s, o_cmp, pt_flat, idx_flat, layer,
                            n_pool, n_pages)
    o5 = o_all.reshape(db, NSA_GROUPS, HEADS_PER_GROUP, NSA_GROUPS, NSA_DIM)
    o_nsa = jnp.stack([o5[:, g, :, g, :] for g in range(NSA_GROUPS)], axis=1).reshape(1, db, NSA_QW)

    y = _merge(x.reshape(1, db, d), o_sb.reshape(1, db, SB_W).astype(BF16), o_nsa.astype(BF16), pr["mg"],
               wl["w_up_sb"], wl["w_up_nsa"], wl["w_out"], wl["norm2_g"], wl["w_ff_in"], wl["w_ff_out"],
               norm_f_g, final_norm, db)
    n_buf = win_k.shape[1]
    n_keep = min(WINDOW, past + n_new)
    new_state = lambda name, groups: _token_major(pr[name][0][None], groups)[0][:, None]
    wk_all = jnp.concatenate([win_k, new_state("wkt", NSA_GROUPS)], axis=1)
    wv_all = jnp.concatenate([win_v, new_state("wvt", NSA_GROUPS)], axis=1)
    state = (new_state("skt", SB_HEADS), new_state("svt", SB_HEADS),
             new_state("ckt", NSA_GROUPS), new_state("cvt", NSA_GROUPS),
             new_state("lkt", NSA_GROUPS), new_state("lvt", NSA_GROUPS),
             wk_all[:, n_buf + n_new - n_keep:], wv_all[:, n_buf + n_new - n_keep:])
    return y.reshape(db, n_new, d), state


def kernel(x_prompt, x_sample, cache_sb_k, cache_sb_v, cache_cmp_k, cache_cmp_v, cache_slc_k, cache_slc_v,
           state_win_k, state_win_v, page_table, norm1_g, w_in, cmp_pool_k, cmp_pool_v, cmp_proj_k, cmp_proj_v,
           w_up_sb, w_up_nsa, w_out, norm2_g, w_ff_in, w_ff_out, norm_f_g):
    depth = w_in.shape[0]
    d_model = x_prompt.shape[-1]
    n_pool = cache_sb_k.shape[1]
    h_p, h_s = x_prompt, x_sample
    pages = lambda c: _channel_major(c).reshape((depth * n_pool,) + (c.shape[3] * c.shape[4], c.shape[2]))
    caches_t = tuple(pages(c) for c in (cache_sb_k, cache_sb_v, cache_cmp_k, cache_cmp_v, cache_slc_k, cache_slc_v))
    new_p, new_s = [], []
    for l in range(depth):
        w_main, w_t = _pack_proj_weights(w_in[l], d_model)
        wl = dict(norm1_g=norm1_g[l], w_main=w_main, w_t=w_t,
                  pool_k=cmp_pool_k[l].reshape(CMP_LEN, NSA_KVW), pool_v=cmp_pool_v[l].reshape(CMP_LEN, NSA_KVW),
                  wbd_k=_block_diag(cmp_proj_k[l]), wbd_v=_block_diag(cmp_proj_v[l]),
                  w_up_sb=w_up_sb[l].astype(BF16), w_up_nsa=w_up_nsa[l].astype(BF16), w_out=w_out[l].astype(BF16),
                  norm2_g=norm2_g[l], w_ff_in=w_ff_in[l].astype(BF16), w_ff_out=w_ff_out[l].astype(BF16))
        last = l == depth - 1
        h_p, st_p = _prompt_layer(h_p, wl, norm_f_g, last)
        h_s, st_s = _sample_layer(h_s, l, caches_t, n_pool, state_win_k[l], state_win_v[l], page_table, wl,
                                  norm_f_g, last)
        new_p.append(st_p)
        new_s.append(st_s)
    stk = lambda lst, j: jnp.stack([st[j] for st in lst])
    return (h_p, h_s) + tuple(stk(new_p, j) for j in range(8)) + tuple(stk(new_s, j) for j in range(8))
```

```python
import functools

import numpy as np
import jax
import jax.numpy as jnp
from jax import lax
from jax.experimental import pallas as pl
from jax.experimental.pallas import tpu as pltpu

SB_HEADS = 8
SB_DIM = 64
NSA_HEADS = 8
NSA_GROUPS = 2
NSA_DIM = 64
HEADS_PER_GROUP = NSA_HEADS // NSA_GROUPS
CMP_LEN = 32
CMP_STRIDE = 16
CMP_RATIO = CMP_LEN // CMP_STRIDE
SLC_BLOCK = 64
N_SEL = 16
WINDOW = 512
N_NSA_BRANCH = 3
EPS = 1e-6
NEG = -1e30
FORCE = 1e4
LOWEST = -3e38
LOG2E = 1.4426950408889634

LANES = 128
VMEM_LIMIT = 56 * 1024 * 1024

SB_W = SB_HEADS * SB_DIM
NSA_QW = NSA_HEADS * NSA_DIM
NSA_KVW = NSA_GROUPS * NSA_DIM

F32 = jnp.float32
BF16 = jnp.bfloat16
NT_DIMS = (((1,), (1,)), ((), ()))


def _dot(a, b):
    return jnp.dot(a, b, preferred_element_type=F32)


def _dot_nt(a, b):
    return lax.dot_general(a, b, NT_DIMS, preferred_element_type=F32)


def _split_bf16(x):
    hi = x.astype(BF16)
    lo = (x - hi.astype(F32)).astype(BF16)
    return hi, lo


def _params(*sem):
    return pltpu.CompilerParams(dimension_semantics=sem, vmem_limit_bytes=VMEM_LIMIT)


def _const_spec(shape):
    nd = len(shape)
    return pl.BlockSpec(shape, lambda *_: (0,) * nd, pipeline_mode=pl.Buffered(1))


def _channel_major(t):
    lead = t.shape[:-3]
    n = len(lead)
    t = jnp.transpose(t, tuple(range(n)) + (n + 1, n + 2, n))
    return t.reshape(lead + (t.shape[-3] * t.shape[-2], t.shape[-1]))


def _token_major(t, groups):
    lead = t.shape[:-2]
    n = len(lead)
    t = t.reshape(lead + (groups, t.shape[-2] // groups, t.shape[-1]))
    return jnp.transpose(t, tuple(range(n)) + (n + 2, n, n + 1))


_C_SQ = 0
_C_NQ = _C_SQ + SB_W
_C_CK = _C_NQ + NSA_HEADS * LANES
_C_CV = _C_CK + NSA_KVW
_C_LK = _C_CV + NSA_KVW
_C_WK = _C_LK + NSA_KVW
_C_MG = _C_WK + NSA_KVW
_R_SK = 0
_R_SV = _R_SK + SB_W
_R_CK = _R_SV + SB_W
_R_CV = _R_CK + NSA_KVW
_R_LK = _R_CV + NSA_KVW
_R_LV = _R_LK + NSA_KVW
_R_WK = _R_LV + NSA_KVW
_R_WV = _R_WK + NSA_KVW
_R_NG = _R_WV + NSA_KVW
NG_ROWS = 16
_R_END = _R_NG + NSA_GROUPS * NG_ROWS


def _proj_kernel(x_ref, g_ref, w_ref, wt_ref,
                 sq_o, nq_o, ck_o, cv_o, lkb_o, wkb_o, mg_o,
                 skt_o, sktb_o, svt_o, svtb_o, ckt_o, cvt_o, lkt_o, lvt_o, lvtb_o, wkt_o, wvt_o, wvtb_o, ngt_o,
                 *, d_mg):
    x = x_ref[0]
    ms = jnp.mean(x * x, axis=-1, keepdims=True)
    xn = (x * lax.rsqrt(ms + EPS) * g_ref[...]).astype(BF16)

    def cols(c0, width):
        return _dot(xn, w_ref[:, c0:c0 + width])

    all_rows = _dot_nt(wt_ref[...], xn)

    def rows(r0, height):
        return all_rows[r0:r0 + height]

    sq_o[0] = (cols(_C_SQ, SB_W) * (SB_DIM ** -0.5 * LOG2E)).astype(BF16)
    nq_o[0] = (cols(_C_NQ, NSA_HEADS * LANES) * (NSA_DIM ** -0.5 * LOG2E)).astype(BF16)
    ck_o[0] = cols(_C_CK, NSA_KVW)
    cv_o[0] = cols(_C_CV, NSA_KVW)
    lkb_o[0] = cols(_C_LK, NSA_KVW).astype(BF16)
    wkb_o[0] = cols(_C_WK, NSA_KVW).astype(BF16)
    mg_o[0] = cols(_C_MG, d_mg)
    skt = rows(_R_SK, SB_W)
    skt_o[0] = skt
    sktb_o[0] = skt.astype(BF16)
    svt = rows(_R_SV, SB_W)
    svt_o[0] = svt
    svtb_o[0] = svt.astype(BF16)
    ckt_o[0] = rows(_R_CK, NSA_KVW)
    cvt_o[0] = rows(_R_CV, NSA_KVW)
    lkt_o[0] = rows(_R_LK, NSA_KVW)
    lvt = rows(_R_LV, NSA_KVW)
    lvt_o[0] = lvt
    lvtb_o[0] = lvt.astype(BF16)
    wkt_o[0] = rows(_R_WK, NSA_KVW)
    wvt = rows(_R_WV, NSA_KVW)
    wvt_o[0] = wvt
    wvtb_o[0] = wvt.astype(BF16)
    ngt_o[0] = rows(_R_NG, NSA_GROUPS * NG_ROWS)


def _pack_proj_weights(w_in, d_model):
    splits = np.cumsum([SB_W, SB_W, SB_W, NSA_QW, NSA_KVW, NSA_KVW, NSA_KVW, NSA_KVW, NSA_KVW, NSA_KVW,
                        NSA_HEADS * N_NSA_BRANCH])
    sq, sk, sv, nq, ck, cv, lk, lv, wk, wv, ng, mg = jnp.split(w_in, splits.tolist(), axis=1)
    nq4 = nq.reshape(d_model, NSA_GROUPS, HEADS_PER_GROUP, NSA_DIM)
    nq_pad = jnp.zeros((d_model, NSA_GROUPS, HEADS_PER_GROUP, NSA_GROUPS, NSA_DIM), w_in.dtype)
    for g in range(NSA_GROUPS):
        nq_pad = nq_pad.at[:, g, :, g, :].set(nq4[:, g])
    nq_pad = nq_pad.reshape(d_model, NSA_HEADS * LANES)
    w_main = jnp.concatenate([sq, nq_pad, ck, cv, lk, wk, mg], axis=1).astype(BF16)
    ng3 = ng.reshape(d_model, NSA_GROUPS, HEADS_PER_GROUP * N_NSA_BRANCH)
    ng3 = jnp.pad(ng3, ((0, 0), (0, 0), (0, NG_ROWS - HEADS_PER_GROUP * N_NSA_BRANCH)))
    w_t = jnp.concatenate([sk, sv, ck, cv, lk, lv, wk, wv, ng3.reshape(d_model, NSA_GROUPS * NG_ROWS)],
                          axis=1).T.astype(BF16)
    return w_main, w_t


def _project(x, norm_g, w_main, w_t, tm):
    b, s, d = x.shape
    d_mg = w_main.shape[1] - _C_MG
    row = lambda width: pl.BlockSpec((1, tm, width), lambda bi, i: (bi, i, 0))
    col = lambda rows: pl.BlockSpec((1, rows, tm), lambda bi, i: (bi, 0, i))
    tok = lambda width, dt: jax.ShapeDtypeStruct((b, s, width), dt)
    chn = lambda rows, dt: jax.ShapeDtypeStruct((b, rows, s), dt)
    out_shape = (tok(SB_W, BF16), tok(NSA_HEADS * LANES, BF16), tok(NSA_KVW, F32), tok(NSA_KVW, F32),
                 tok(NSA_KVW, BF16), tok(NSA_KVW, BF16), tok(d_mg, F32),
                 chn(SB_W, F32), chn(SB_W, BF16), chn(SB_W, F32), chn(SB_W, BF16),
                 chn(NSA_KVW, F32), chn(NSA_KVW, F32), chn(NSA_KVW, F32), chn(NSA_KVW, F32), chn(NSA_KVW, BF16),
                 chn(NSA_KVW, F32), chn(NSA_KVW, F32), chn(NSA_KVW, BF16), chn(NSA_GROUPS * NG_ROWS, F32))
    out_specs = (row(SB_W), row(NSA_HEADS * LANES), row(NSA_KVW), row(NSA_KVW), row(NSA_KVW), row(NSA_KVW),
                 row(d_mg),
                 col(SB_W), col(SB_W), col(SB_W), col(SB_W),
                 col(NSA_KVW), col(NSA_KVW), col(NSA_KVW), col(NSA_KVW), col(NSA_KVW),
                 col(NSA_KVW), col(NSA_KVW), col(NSA_KVW), col(NSA_GROUPS * NG_ROWS))
    names = ("sq", "nq", "ck", "cv", "lkb", "wkb", "mg", "skt", "sktb", "svt", "svtb", "ckt", "cvt", "lkt", "lvt",
             "lvtb", "wkt", "wvt", "wvtb", "ngt")
    outs = pl.pallas_call(
        functools.partial(_proj_kernel, d_mg=d_mg),
        grid=(b, s // tm),
        in_specs=[row(d), _const_spec((1, d)), _const_spec(w_main.shape), _const_spec(w_t.shape)],
        out_specs=out_specs,
        out_shape=out_shape,
        compiler_params=_params("parallel", "parallel"),
        name="in_projection",
    )(x, norm_g.reshape(1, d), w_main, w_t)
    return dict(zip(names, outs))


def _compress_kernel(ck_ref, cv_ref, pk_ref, pv_ref, wk_ref, wvt_ref, kc_o, vct_o):
    n_sub = ck_ref.shape[1] // CMP_STRIDE

    def pooled(src, pw_ref):
        pw = pw_ref[...]
        a = jnp.zeros((n_sub, NSA_KVW), F32)
        b = jnp.zeros((n_sub, NSA_KVW), F32)
        for r in range(CMP_STRIDE):
            rows = src[0, pl.ds(r, n_sub, stride=CMP_STRIDE), :]
            a = a + rows * pw[r:r + 1, :]
            b = b + rows * pw[CMP_STRIDE + r:CMP_STRIDE + r + 1, :]
        return a + pltpu.roll(b, n_sub - 1, axis=0)

    kc_o[0] = _dot(pooled(ck_ref, pk_ref).astype(BF16), wk_ref[...]).astype(BF16)
    vct_o[0] = _dot_nt(wvt_ref[...], pooled(cv_ref, pv_ref).astype(BF16)).astype(BF16)


def _block_diag(proj):
    out = jnp.zeros((NSA_KVW, NSA_KVW), proj.dtype)
    for g in range(NSA_GROUPS):
        out = out.at[g * NSA_DIM:(g + 1) * NSA_DIM, g * NSA_DIM:(g + 1) * NSA_DIM].set(proj[g])
    return out


def _compress_prompt(ck, cv, pool_k, pool_v, wbd_k, wbd_v):
    b, s, _ = ck.shape
    n_sub = s // CMP_STRIDE
    kv = pl.BlockSpec((1, s, NSA_KVW), lambda bi: (bi, 0, 0))
    return pl.pallas_call(
        _compress_kernel,
        grid=(b,),
        in_specs=[kv, kv, _const_spec((CMP_LEN, NSA_KVW)), _const_spec((CMP_LEN, NSA_KVW)),
                  _const_spec((NSA_KVW, NSA_KVW)), _const_spec((NSA_KVW, NSA_KVW))],
        out_specs=(pl.BlockSpec((1, n_sub, NSA_KVW), lambda bi: (bi, 0, 0)),
                   pl.BlockSpec((1, NSA_KVW, n_sub), lambda bi: (bi, 0, 0))),
        out_shape=(jax.ShapeDtypeStruct((b, n_sub, NSA_KVW), BF16),
                   jax.ShapeDtypeStruct((b, NSA_KVW, n_sub), BF16)),
        compiler_params=_params("parallel"),
        name="compress_prompt",
    )(ck, cv, pool_k, pool_v, wbd_k.astype(BF16), wbd_v.T.astype(BF16))


def _softplus2(u):
    return jnp.maximum(u, 0.0) + jnp.log2(1.0 + jnp.exp2(-jnp.abs(u)))


def _strict_lower(n):
    return jnp.asarray(np.tril(np.ones((n, n), np.float32), -1), BF16)


SB_UNROLL = 4


def _sb_prompt_kernel(q_ref, kt_ref, vt_ref, tri_ref, o_ref, r_ref, acc_ref, *, t):
    i = pl.program_id(2)
    q = q_ref[0]
    lane = lax.broadcasted_iota(jnp.int32, (t, LANES), 1)
    zero = jnp.zeros_like(q)
    q2 = jnp.concatenate([jnp.where(lane < SB_DIM, q, zero), jnp.where(lane >= SB_DIM, q, zero)], axis=0)
    tri = tri_ref[...]
    key_i = lax.broadcasted_iota(jnp.int32, (2 * t, t), 1)
    row_i = lax.broadcasted_iota(jnp.int32, (2 * t, t), 0)
    causal = key_i < (row_i & (t - 1))

    def group(tile_ids, first_is_diag):
        parts = []
        for n, j in enumerate(tile_ids):
            off = pl.multiple_of(j * t, t)
            z = _dot(q2, kt_ref[0, :, pl.ds(off, t)])
            sp = _softplus2(z)
            diag = first_is_diag and n == 0
            drop = jnp.where(causal, sp, 0.0) if diag else sp
            local = _dot(drop.astype(BF16), tri)
            parts.append((off, diag, z - sp - local, jnp.sum(drop, axis=-1, keepdims=True)))
        r = r_ref[...]
        acc = acc_ref[...]
        for off, diag, arg, total in parts:
            a = jnp.exp2(arg + r)
            if diag:
                a = jnp.where(causal, a, 0.0)
            acc = acc + _dot_nt(a.astype(BF16), vt_ref[0, :, pl.ds(off, t)])
            r = r - total
        r_ref[...] = r
        acc_ref[...] = acc

    r_ref[...] = jnp.zeros_like(r_ref)
    acc_ref[...] = jnp.zeros_like(acc_ref)
    extra = i & (SB_UNROLL - 1)
    for n_extra in range(SB_UNROLL):
        @pl.when(extra == n_extra)
        def _(n_extra=n_extra):
            group(tuple(i - n for n in range(n_extra + 1)), True)

    top = i - 1 - extra

    def full_group(p, carry):
        group(tuple(top - SB_UNROLL * p - n for n in range(SB_UNROLL)), False)
        return carry

    lax.fori_loop(0, i // SB_UNROLL, full_group, 0)
    o_ref[0] = jnp.where(lane < SB_DIM, acc_ref[0:t, :], acc_ref[t:2 * t, :]).astype(o_ref.dtype)


def _sb_prompt(sq, skt, svt, t):
    b, s, w = sq.shape
    qo = pl.BlockSpec((1, t, LANES), lambda bi, hp, i: (bi, i, hp))
    kv = pl.BlockSpec((1, LANES, s), lambda bi, hp, i: (bi, hp, 0))
    return pl.pallas_call(
        functools.partial(_sb_prompt_kernel, t=t),
        grid=(b, w // LANES, s // t),
        in_specs=[qo, kv, kv, _const_spec((t, t))],
        out_specs=qo,
        out_shape=jax.ShapeDtypeStruct((b, s, w), BF16),
        scratch_shapes=[pltpu.VMEM((2 * t, 1), F32), pltpu.VMEM((2 * t, LANES), F32)],
        compiler_params=_params("parallel", "parallel", "arbitrary"),
        name="sb_prompt",
    )(sq, skt, svt, _strict_lower(t))


def _sb_sample_kernel(pt_ref, q_ref, *refs, n_pc):
    k_refs = refs[:n_pc]
    v_refs = refs[n_pc:2 * n_pc]
    tri_ref = refs[2 * n_pc]
    o_ref = refs[2 * n_pc + 1]
    r_ref, acc_ref = refs[2 * n_pc + 2:]
    c = pl.program_id(1)
    page = k_refs[0].shape[-1]

    @pl.when(c == 0)
    def _():
        r_ref[...] = jnp.zeros_like(r_ref)
        acc_ref[...] = jnp.zeros_like(acc_ref)

    q_col = q_ref[0].reshape(SB_HEADS, SB_DIM, page)
    scores = []
    for ii in range(n_pc):
        kt = k_refs[ii][0].reshape(SB_HEADS, SB_DIM, page)
        scores.append(jnp.sum(kt * q_col, axis=1))
    z = jnp.concatenate(scores, axis=0)
    sp = _softplus2(z)
    log_keep = -sp
    hi, lo = _split_bf16(log_keep)
    both = _dot(jnp.concatenate([hi, lo], axis=0), tri_ref[...])
    arg = z - sp + both[:n_pc * SB_HEADS] + both[n_pc * SB_HEADS:]
    totals = jnp.sum(log_keep, axis=-1, keepdims=True)
    r = r_ref[:, 0:1]
    acc = acc_ref[...]
    for ii in range(n_pc):
        rows = slice(ii * SB_HEADS, (ii + 1) * SB_HEADS)
        a = jnp.exp2(arg[rows] + r)
        vt = v_refs[ii][0].reshape(SB_HEADS, SB_DIM, page)
        acc = acc + vt * a[:, None, :]
        r = r + totals[rows]
    r_ref[...] = jnp.broadcast_to(r, r_ref.shape)
    acc_ref[...] = acc

    @pl.when(c == pl.num_programs(1) - 1)
    def _():
        o_ref[0] = jnp.sum(acc, axis=-1, keepdims=True).reshape(SB_W, 1)


def _sb_sample(sq_s, cache_kt, cache_vt, pt_flat, layer, n_pool, n_pages, n_pc):
    db = sq_s.shape[0]
    page = cache_kt.shape[-1]
    n_chunks = n_pages // n_pc

    def page_spec(ii):
        def imap(bi, c, pt):
            p = n_pages - 1 - (c * n_pc + ii)
            return (layer * n_pool + jnp.clip(pt[bi * n_pages + p], 0, n_pool - 1), 0, 0)
        return pl.BlockSpec((1, SB_W, page), imap)

    specs = [pl.BlockSpec((1, SB_W, page), lambda bi, c, pt: (bi, 0, 0))]
    specs += [page_spec(ii) for ii in range(n_pc)] * 2
    specs += [pl.BlockSpec((page, page), lambda bi, c, pt: (0, 0))]
    grid_spec = pltpu.PrefetchScalarGridSpec(
        num_scalar_prefetch=1, grid=(db, n_chunks), in_specs=specs,
        out_specs=pl.BlockSpec((1, SB_W, 1), lambda bi, c, pt: (bi, 0, 0)),
        scratch_shapes=[pltpu.VMEM((SB_HEADS, LANES), F32), pltpu.VMEM((SB_HEADS, SB_DIM, page), F32)])
    q_lanes = jnp.broadcast_to(sq_s.astype(F32)[:, :, None], (db, SB_W, page))
    out = pl.pallas_call(
        functools.partial(_sb_sample_kernel, n_pc=n_pc),
        grid_spec=grid_spec,
        out_shape=jax.ShapeDtypeStruct((db, SB_W, 1), F32),
        compiler_params=_params("parallel", "arbitrary"),
        name="sb_sample",
    )(pt_flat, q_lanes, *([cache_kt] * n_pc), *([cache_vt] * n_pc), _strict_lower(page))
    return out.reshape(db, SB_W)


def _rank_desc(v):
    n = v.shape[0]
    sub = lax.broadcasted_iota(jnp.int32, (8, v.shape[1]), 0)
    cnt = jnp.zeros(v.shape, jnp.int32)
    for m in range(n):
        rowv = v[m:m + 1, :]
        g0 = (m // 8) * 8
        parts = []
        if g0 > 0:
            parts.append(jnp.where(rowv > v[:g0], 1, 0))
        blk = v[g0:g0 + 8]
        parts.append(jnp.where(sub > (m % 8), jnp.where(rowv >= blk, 1, 0), jnp.where(rowv > blk, 1, 0)))
        if g0 + 8 < n:
            parts.append(jnp.where(rowv >= v[g0 + 8:], 1, 0))
        cnt = cnt + (parts[0] if len(parts) == 1 else jnp.concatenate(parts, axis=0))
    return cnt


def _cmp_to_slc_t(n_cmp, n_slc, n_cmp_pad, n_slc_pad):
    c_start = np.arange(n_cmp) * CMP_STRIDE
    s_start = np.arange(n_slc) * SLC_BLOCK
    inter = (np.minimum(c_start[None, :] + CMP_LEN, s_start[:, None] + SLC_BLOCK)
             - np.maximum(c_start[None, :], s_start[:, None]))
    ov = np.zeros((n_slc_pad, n_cmp_pad), np.float32)
    ov[:n_slc, :n_cmp] = np.clip(inter, 0, None) / CMP_LEN
    return ov


def _alibi_slopes():
    return (2.0 ** (-8.0 * (np.arange(NSA_HEADS) + 1) / NSA_HEADS) * LOG2E).astype(np.float32)


def _nsa_prompt_kernel(q_ref, lk_ref, lvt_ref, wk_ref, wvt_ref, kc_ref, vct_ref, ngt_ref, slope_ref, ovt_ref,
                       o_ref, sel_ref, bias_ref, *, tq, tks, tkw, n_sel):
    i = pl.program_id(2)
    nh = HEADS_PER_GROUP
    w = nh * tq
    q0 = i * tq
    qs = jnp.concatenate([q_ref[0, :, h * LANES:(h + 1) * LANES] for h in range(nh)], axis=0)
    lane = lax.broadcasted_iota(jnp.int32, (1, w), 1)
    qpos = q0 + (lane & (tq - 1))
    slope = slope_ref[0]

    kc = kc_ref[0]
    n_c = kc.shape[0]
    c_end = lax.broadcasted_iota(jnp.int32, (n_c, 1), 0) * CMP_STRIDE + (CMP_LEN - 1)
    dist = qpos - c_end
    valid = dist >= 0
    s = _dot_nt(kc, qs) - slope * dist.astype(F32)
    sm = jnp.where(valid, s, NEG)
    m = jnp.max(sm, axis=0, keepdims=True)
    e = jnp.where(valid, jnp.exp2(sm - m), 0.0)
    l = jnp.sum(e, axis=0, keepdims=True)
    p = e * (1.0 / jnp.where(l > 0.0, l, 1.0))
    o_cmp = _dot(vct_ref[0], p.astype(BF16))

    p_sum = p[:, 0:tq]
    for h in range(1, nh):
        p_sum = p_sum + p[:, h * tq:(h + 1) * tq]
    hi, lo = _split_bf16(p_sum)
    ovt = ovt_ref[...]
    imp = _dot(ovt, hi) + _dot(ovt, lo)
    n_slc = imp.shape[0]
    blk = lax.broadcasted_iota(jnp.int32, (n_slc, tq), 0)
    cur = (q0 + lax.broadcasted_iota(jnp.int32, (1, tq), 1)) // SLC_BLOCK
    forced = (blk == 0) | (blk == cur) | (blk == cur - 1)
    imp = jnp.where(forced, FORCE, jnp.where(blk <= cur, imp, -FORCE))
    n_live = (q0 + tq - 1) // SLC_BLOCK + 1
    sizes = [r for r in (32, 64) if n_sel <= r < n_slc] + [n_slc]
    for k, rows in enumerate(sizes):
        lo = sizes[k - 1] if k else 0
        hit = (n_live > lo) if k == len(sizes) - 1 else ((n_live > lo) & (n_live <= rows))

        @pl.when(hit)
        def _(rows=rows):
            head = jnp.where(_rank_desc(imp[:rows]) < n_sel, 0.0, NEG)
            if rows < n_slc:
                head = jnp.concatenate([head, jnp.full((n_slc - rows, tq), NEG, F32)], axis=0)
            sel_ref[...] = head

    bias_ref[...] = slope * lax.broadcasted_iota(jnp.int32, (tks, 1), 0).astype(F32)

    def sel_tile(j, carry, causal):
        m_old, l_old, acc = carry
        off = pl.multiple_of(j * tks, tks)
        rows = []
        for bb in range(tks // SLC_BLOCK):
            r = sel_ref[pl.ds(j * (tks // SLC_BLOCK) + bb, 1), :]
            rows.append(jnp.broadcast_to(r, (SLC_BLOCK, tq)))
        picked = jnp.concatenate(rows, axis=0)
        picked = jnp.concatenate([picked] * nh, axis=1)
        sm = _dot_nt(lk_ref[0, pl.ds(off, tks), :], qs) + bias_ref[...] + picked
        if causal:
            t_pos = off + lax.broadcasted_iota(jnp.int32, (tks, 1), 0)
            sm = jnp.where(qpos >= t_pos, sm, NEG)
        shift = slope * off.astype(F32)
        m_new = jnp.maximum(m_old, jnp.max(sm, axis=0, keepdims=True) + shift)
        alpha = jnp.exp2(m_old - m_new)
        p_t = jnp.exp2(sm - (m_new - shift))
        l_new = alpha * l_old + jnp.sum(p_t, axis=0, keepdims=True)
        acc = alpha * acc + _dot(lvt_ref[0, :, pl.ds(off, tks)], p_t.astype(BF16))
        return m_new, l_new, acc

    init = (jnp.full((1, w), NEG, F32), jnp.zeros((1, w), F32), jnp.zeros((NSA_DIM, w), F32))
    n_full = q0 // tks
    carry = lax.fori_loop(0, n_full, lambda j, c: sel_tile(j, c, False), init)
    _, l_sel, acc_sel = sel_tile(n_full, carry, True)

    win_scores = []
    for d in range(WINDOW // tkw + 1):
        jt = i - d
        off = pl.multiple_of(jnp.maximum(jt, 0) * tkw, tkw)
        limit = jnp.where(jt >= 0, WINDOW, -1)
        dist_t = qpos - (off + lax.broadcasted_iota(jnp.int32, (tkw, 1), 0))
        valid_t = jnp.where(dist_t <= limit, dist_t, -1) >= 0
        s_t = _dot_nt(wk_ref[0, pl.ds(off, tkw), :], qs) - slope * dist_t.astype(F32)
        win_scores.append((off, jnp.where(valid_t, s_t, NEG)))
    m_win = jnp.max(win_scores[0][1], axis=0, keepdims=True)
    for _, sm_t in win_scores[1:]:
        m_win = jnp.maximum(m_win, jnp.max(sm_t, axis=0, keepdims=True))
    l_win = jnp.zeros((1, w), F32)
    acc_win = jnp.zeros((NSA_DIM, w), F32)
    for off, sm_t in win_scores:
        p_t = jnp.exp2(sm_t - m_win)
        l_win = l_win + jnp.sum(p_t, axis=0, keepdims=True)
        acc_win = acc_win + _dot(wvt_ref[0, :, pl.ds(off, tkw)], p_t.astype(BF16))

    gates = jax.nn.sigmoid(ngt_ref[0])
    o_sel = acc_sel * (1.0 / l_sel)
    o_win = acc_win * (1.0 / l_win)
    heads = []
    for h in range(nh):
        hs = slice(h * tq, (h + 1) * tq)
        g0 = gates[h * N_NSA_BRANCH + 0:h * N_NSA_BRANCH + 1, :]
        g1 = gates[h * N_NSA_BRANCH + 1:h * N_NSA_BRANCH + 2, :]
        g2 = gates[h * N_NSA_BRANCH + 2:h * N_NSA_BRANCH + 3, :]
        heads.append(g0 * o_cmp[:, hs] + g1 * o_sel[:, hs] + g2 * o_win[:, hs])
    for hp in range(nh // 2):
        pair = jnp.concatenate([heads[2 * hp], heads[2 * hp + 1]], axis=0)
        o_ref[0, :, hp * LANES:(hp + 1) * LANES] = pair.T.astype(o_ref.dtype)


def _nsa_prompt(nq_pad, lk_b, lvt, wk_b, wvt, kc, vct, ngt, tq, tks):
    b, s, _ = nq_pad.shape
    assert tq == LANES and s % tks == 0 and tks % SLC_BLOCK == 0
    n_c = kc.shape[1]
    n_slc = s // SLC_BLOCK
    n_cmp = n_c - CMP_RATIO + 1
    w = HEADS_PER_GROUP * tq
    slopes = np.repeat(_alibi_slopes().reshape(NSA_GROUPS, HEADS_PER_GROUP), tq, axis=1).reshape(NSA_GROUPS, 1, w)
    ovt = jnp.asarray(_cmp_to_slc_t(n_cmp, n_slc, n_c, n_slc), BF16)
    full = lambda rows, lanes: pl.BlockSpec((1, rows, lanes), lambda bi, g, i: (bi, 0, 0))
    grp = lambda rows, lanes: pl.BlockSpec((1, rows, lanes), lambda bi, g, i: (bi, g, 0))
    return pl.pallas_call(
        functools.partial(_nsa_prompt_kernel, tq=tq, tks=tks, tkw=tq, n_sel=min(N_SEL, n_slc)),
        grid=(b, NSA_GROUPS, s // tq),
        in_specs=[pl.BlockSpec((1, tq, HEADS_PER_GROUP * LANES), lambda bi, g, i: (bi, i, g)),
                  full(s, NSA_KVW), grp(NSA_DIM, s), full(s, NSA_KVW), grp(NSA_DIM, s),
                  full(n_c, NSA_KVW), grp(NSA_DIM, n_c),
                  pl.BlockSpec((1, NG_ROWS, tq), lambda bi, g, i: (bi, g, i)),
                  pl.BlockSpec((1, 1, w), lambda bi, g, i: (g, 0, 0)),
                  _const_spec((n_slc, n_c))],
        out_specs=pl.BlockSpec((1, tq, HEADS_PER_GROUP * NSA_DIM), lambda bi, g, i: (bi, i, g)),
        out_shape=jax.ShapeDtypeStruct((b, s, NSA_QW), BF16),
        scratch_shapes=[pltpu.VMEM((n_slc, tq), F32), pltpu.VMEM((tks, w), F32)],
        compiler_params=_params("parallel", "parallel", "arbitrary"),
        name="nsa_prompt",
    )(nq_pad, lk_b, lvt, wk_b, wvt, kc, vct, ngt, jnp.asarray(slopes), ovt)


def _nsa_cmp_sample_kernel(pt_ref, q_ref, *refs, n_pc, past):
    k_refs = refs[:n_pc]
    v_refs = refs[n_pc:2 * n_pc]
    pk0_ref, pk1_ref, pv0_ref, pv1_ref, seg_ref, wkt_ref, wvt_ref, slope_ref, ov_ref = refs[2 * n_pc:2 * n_pc + 9]
    oc_o, imp_o = refs[2 * n_pc + 9:2 * n_pc + 11]
    ak_ref, bk_ref, av_ref, bv_ref = refs[2 * n_pc + 11:]
    c = pl.program_id(1)
    page = k_refs[0].shape[-1]
    m_chunk = n_pc * (page // CMP_STRIDE)

    off = pl.multiple_of(c * m_chunk, m_chunk)
    for page_refs, w0_ref, w1_ref, a_ref, b_ref in ((k_refs, pk0_ref, pk1_ref, ak_ref, bk_ref),
                                                    (v_refs, pv0_ref, pv1_ref, av_ref, bv_ref)):
        w0 = w0_ref[...]
        w1 = w1_ref[...]
        pooled = jnp.zeros((2 * NSA_KVW, m_chunk), F32)
        for pair in range(n_pc // 2):
            xa = page_refs[2 * pair][0]
            xb = page_refs[2 * pair + 1][0]
            lhs = jnp.concatenate([jnp.concatenate([xa * w0, xb * w0], axis=1),
                                   jnp.concatenate([xa * w1, xb * w1], axis=1)], axis=0).astype(BF16)
            pooled = pooled + _dot(lhs, seg_ref[pair])
        a_ref[:, pl.ds(off, m_chunk)] = pooled[:NSA_KVW]
        b_ref[:, pl.ds(off, m_chunk)] = pooled[NSA_KVW:]

    @pl.when(c == pl.num_programs(1) - 1)
    def _():
        n_sub = ak_ref.shape[1]
        pooled_k = ak_ref[...] + pltpu.roll(bk_ref[...], n_sub - 1, axis=1)
        pooled_v = av_ref[...] + pltpu.roll(bv_ref[...], n_sub - 1, axis=1)
        kct = _dot(wkt_ref[...], pooled_k.astype(BF16)).astype(BF16)
        vct = _dot(wvt_ref[...], pooled_v.astype(BF16)).astype(BF16)
        qn = q_ref[0].astype(BF16)
        c_end = lax.broadcasted_iota(jnp.int32, (1, n_sub), 1) * CMP_STRIDE + (CMP_LEN - 1)
        dist = past - c_end
        valid = dist >= 0
        s = _dot(qn, kct) - slope_ref[:, 0:1] * dist.astype(F32)
        sm = jnp.where(valid, s, NEG)
        m = jnp.max(sm, axis=-1, keepdims=True)
        e = jnp.where(valid, jnp.exp2(sm - m), 0.0)
        l = jnp.sum(e, axis=-1, keepdims=True)
        p = e * (1.0 / jnp.where(l > 0.0, l, 1.0))
        oc_o[0] = _dot_nt(p.astype(BF16), vct)
        sums = [jnp.sum(p[g * HEADS_PER_GROUP:(g + 1) * HEADS_PER_GROUP], axis=0, keepdims=True)
                for g in range(NSA_GROUPS)]
        sums.append(jnp.zeros((NSA_HEADS - NSA_GROUPS, n_sub), F32))
        hi, lo = _split_bf16(jnp.concatenate(sums, axis=0))
        ov = ov_ref[...]
        imp_o[0] = _dot(hi, ov) + _dot(lo, ov)


def _nsa_cmp_sample(qn_s, cache_kt, cache_vt, pt_flat, layer, n_pool, n_pages, n_pc, pool_k, pool_v, wbd_k, wbd_v,
                    n_slc_pad):
    db = qn_s.shape[0]
    page = cache_kt.shape[-1]
    past = n_pages * page
    n_sub = past // CMP_STRIDE
    n_cmp = n_sub - CMP_RATIO + 1
    n_slc = -(-(past + 1) // SLC_BLOCK)
    sub_per_page = page // CMP_STRIDE
    m_chunk = n_pc * sub_per_page
    assert n_pc % 2 == 0 and m_chunk % LANES == 0
    ov = jnp.asarray(_cmp_to_slc_t(n_cmp, n_slc, n_sub, n_slc_pad).T, BF16)
    slopes = jnp.asarray(np.repeat(_alibi_slopes()[:, None], LANES, axis=1))
    t_idx = np.arange(2 * page)
    seg = np.zeros((n_pc // 2, 2 * page, m_chunk), np.float32)
    for pair in range(n_pc // 2):
        seg[pair, t_idx, 2 * sub_per_page * pair + t_idx // CMP_STRIDE] = 1.0
    tile_w = lambda pw, half: jnp.tile(pw[half * CMP_STRIDE:(half + 1) * CMP_STRIDE].T, (1, sub_per_page))

    def page_spec(ii):
        return pl.BlockSpec((1, NSA_KVW, page),
                            lambda bi, c, pt: (layer * n_pool
                                               + jnp.clip(pt[bi * n_pages + c * n_pc + ii], 0, n_pool - 1), 0, 0))

    cst = lambda shape: pl.BlockSpec(shape, lambda bi, c, pt: (0,) * len(shape))
    specs = [pl.BlockSpec((1, NSA_HEADS, LANES), lambda bi, c, pt: (bi, 0, 0))]
    specs += [page_spec(ii) for ii in range(n_pc)] * 2
    specs += [cst((NSA_KVW, page))] * 4
    specs += [cst(seg.shape), cst((NSA_KVW, NSA_KVW)), cst((NSA_KVW, NSA_KVW)),
              cst((NSA_HEADS, LANES)), cst((n_sub, n_slc_pad))]
    out_blk = lambda lanes: pl.BlockSpec((1, NSA_HEADS, lanes), lambda bi, c, pt: (bi, 0, 0))
    grid_spec = pltpu.PrefetchScalarGridSpec(
        num_scalar_prefetch=1, grid=(db, n_pages // n_pc), in_specs=specs,
        out_specs=(out_blk(LANES), out_blk(n_slc_pad)),
        scratch_shapes=[pltpu.VMEM((NSA_KVW, n_sub), F32)] * 4)
    return pl.pallas_call(
        functools.partial(_nsa_cmp_sample_kernel, n_pc=n_pc, past=past),
        grid_spec=grid_spec,
        out_shape=(jax.ShapeDtypeStruct((db, NSA_HEADS, LANES), F32),
                   jax.ShapeDtypeStruct((db, NSA_HEADS, n_slc_pad), F32)),
        compiler_params=_params("parallel", "arbitrary"),
        name="nsa_cmp_sample",
    )(pt_flat, qn_s, *([cache_kt] * n_pc), *([cache_vt] * n_pc),
      tile_w(pool_k, 0), tile_w(pool_k, 1), tile_w(pool_v, 0), tile_w(pool_v, 1), jnp.asarray(seg, BF16),
      wbd_k.T.astype(BF16), wbd_v.T.astype(BF16), slopes, ov)


def _topk_sample_kernel(imp_ref, idx_o, *, n_slc, cur, n_sel):
    imp = imp_ref[...]
    blk = lax.broadcasted_iota(jnp.int32, imp.shape, 0)
    forced = (blk == 0) | (blk == cur) | (blk == cur - 1)
    imp = jnp.where(forced, FORCE, jnp.where(blk <= cur, imp, -FORCE))
    imp = jnp.where(blk < n_slc, imp, LOWEST)
    rank = _rank_desc(imp)
    rows = [jnp.sum(jnp.where(rank == r, blk, 0), axis=0, keepdims=True) for r in range(n_sel)]
    idx_o[...] = jnp.concatenate(rows, axis=0)


def _topk_sample(imp_t, n_slc, cur, n_sel):
    n_pad, cols = imp_t.shape
    return pl.pallas_call(
        functools.partial(_topk_sample_kernel, n_slc=n_slc, cur=cur, n_sel=n_sel),
        grid=(1,),
        in_specs=[_const_spec((n_pad, cols))],
        out_specs=_const_spec((n_sel, cols)),
        out_shape=jax.ShapeDtypeStruct((n_sel, cols), jnp.int32),
        compiler_params=_params("arbitrary"),
        name="topk_sample",
    )(imp_t)


def _nsa_sel_sample_kernel(pt_ref, idx_ref, q_ref, *refs, n_sel, n_cached, past, n_buf):
    n_blk = NSA_GROUPS * n_sel
    k_refs = refs[:n_blk]
    v_refs = refs[n_blk:2 * n_blk]
    wink_ref, winv_ref, new_ref, ng_ref, oc_ref, slope_ref, o_ref = refs[2 * n_blk:]
    bi = pl.program_id(0)
    page = k_refs[0].shape[-1]
    halves = page // SLC_BLOCK
    qn = q_ref[0]
    qb = qn.astype(BF16)
    rowi = lax.broadcasted_iota(jnp.int32, (NSA_HEADS, 1), 0)
    in_g0 = rowi < HEADS_PER_GROUP
    slope = slope_ref[:, 0:1]
    lk_new, lv_new, wk_new, wv_new = (new_ref[0, r:r + 1, :] for r in range(4))

    def new_score(k_row):
        return jnp.sum(qn * k_row, axis=-1, keepdims=True)

    width = n_sel * page
    lane = lax.broadcasted_iota(jnp.int32, (1, width), 1)
    slot = lane // page
    scores, values, blk_rows, has_new = [], [], [], []
    for g in range(NSA_GROUPS):
        ids = [idx_ref[(bi * NSA_GROUPS + g) * n_sel + r] for r in range(n_sel)]
        scores.append(jnp.concatenate(
            [_dot(qb, k_refs[g * n_sel + r][0].astype(BF16)) for r in range(n_sel)], axis=1))
        values.append(jnp.concatenate([v_refs[g * n_sel + r][0] for r in range(n_sel)], axis=1).astype(BF16))
        row = jnp.zeros((1, width), jnp.int32)
        flag = jnp.int32(0)
        for r in range(n_sel):
            row = jnp.where(slot == r, ids[r], row)
            flag = jnp.maximum(flag, jnp.where(ids[r] >= n_cached, 1, 0))
        blk_rows.append(row)
        has_new.append(flag)
    s = jnp.where(in_g0, scores[0], scores[1])
    blk = jnp.where(in_g0, blk_rows[0], blk_rows[1])
    page_blk = jnp.minimum(blk, n_cached - 1)
    tok = (page_blk // halves) * page + (lane & (page - 1))
    dist = past - tok
    valid = jnp.where(blk < n_cached, tok // SLC_BLOCK, -1) == blk
    s = s - slope * dist.astype(F32)
    sm = jnp.where(valid, s, NEG)
    new_on = jnp.where(in_g0, has_new[0], has_new[1]) > 0
    s_new = jnp.where(new_on, new_score(lk_new), NEG)
    m = jnp.maximum(jnp.max(sm, axis=-1, keepdims=True), s_new)
    e = jnp.where(valid, jnp.exp2(sm - m), 0.0)
    e_new = jnp.where(new_on, jnp.exp2(s_new - m), 0.0)
    l = jnp.sum(e, axis=-1, keepdims=True) + e_new
    eb = e.astype(BF16)
    o_sel = jnp.where(in_g0, _dot_nt(eb, values[0]), _dot_nt(eb, values[1])) + e_new * lv_new
    o_sel = o_sel * (1.0 / l)

    pos = lax.broadcasted_iota(jnp.int32, (1, n_buf), 1)
    dist_w = n_buf - pos
    valid_w = jnp.where(dist_w <= WINDOW, past - dist_w, -1) >= 0
    s_w = _dot(qb, wink_ref[0].astype(BF16)) - slope * dist_w.astype(F32)
    sm_w = jnp.where(valid_w, s_w, NEG)
    s_wn = new_score(wk_new)
    m_w = jnp.maximum(jnp.max(sm_w, axis=-1, keepdims=True), s_wn)
    e_w = jnp.where(valid_w, jnp.exp2(sm_w - m_w), 0.0)
    e_wn = jnp.exp2(s_wn - m_w)
    l_w = jnp.sum(e_w, axis=-1, keepdims=True) + e_wn
    o_win = (_dot_nt(e_w.astype(BF16), winv_ref[0].astype(BF16)) + e_wn * wv_new) * (1.0 / l_w)

    gates = jax.nn.sigmoid(ng_ref[0])
    o = gates[:, 0:1] * oc_ref[0] + gates[:, 1:2] * o_sel + gates[:, 2:3] * o_win
    lane_g = lax.broadcasted_iota(jnp.int32, (NSA_HEADS, LANES), 1) // NSA_DIM
    o_ref[0] = jnp.where(lane_g == rowi // HEADS_PER_GROUP, o, 0.0)


def _nsa_sel_sample(qn_s, cache_kt, cache_vt, win_kt, win_vt, new_rows, ng_s, o_cmp, pt_flat, idx_flat, layer,
                    n_pool, n_pages):
    db = qn_s.shape[0]
    page = cache_kt.shape[-1]
    past = n_pages * page
    halves = page // SLC_BLOCK
    n_cached = past // SLC_BLOCK
    n_sel = idx_flat.shape[0] // (db * NSA_GROUPS)
    n_buf = win_kt.shape[-1]
    slopes = jnp.asarray(np.repeat(_alibi_slopes()[:, None], LANES, axis=1))

    def blk_spec(g, r):
        def imap(bi, pt, idx):
            n = jnp.clip(idx[(bi * NSA_GROUPS + g) * n_sel + r], 0, n_cached - 1)
            return (layer * n_pool + jnp.clip(pt[bi * n_pages + n // halves], 0, n_pool - 1), 0, 0)
        return pl.BlockSpec((1, NSA_KVW, page), imap)

    per_req = lambda rows, lanes: pl.BlockSpec((1, rows, lanes), lambda bi, pt, idx: (bi, 0, 0))
    blocks = [blk_spec(g, r) for g in range(NSA_GROUPS) for r in range(n_sel)]
    specs = [per_req(NSA_HEADS, LANES)] + blocks + blocks
    specs += [per_req(NSA_KVW, n_buf), per_req(NSA_KVW, n_buf), per_req(8, NSA_KVW),
              per_req(NSA_HEADS, LANES), per_req(NSA_HEADS, LANES),
              pl.BlockSpec((NSA_HEADS, LANES), lambda bi, pt, idx: (0, 0))]
    grid_spec = pltpu.PrefetchScalarGridSpec(
        num_scalar_prefetch=2, grid=(db,), in_specs=specs, out_specs=per_req(NSA_HEADS, LANES))
    n_blk = len(blocks)
    return pl.pallas_call(
        functools.partial(_nsa_sel_sample_kernel, n_sel=n_sel, n_cached=n_cached, past=past, n_buf=n_buf),
        grid_spec=grid_spec,
        out_shape=jax.ShapeDtypeStruct((db, NSA_HEADS, LANES), F32),
        compiler_params=_params("parallel"),
        name="nsa_sel_sample",
    )(pt_flat, idx_flat, qn_s, *([cache_kt] * n_blk), *([cache_vt] * n_blk), win_kt, win_vt, new_rows, ng_s, o_cmp,
      slopes)


def _merge_kernel(x_ref, osb_ref, onsa_ref, mg_ref, wsb_ref, wnsa_ref, wout_ref, g2_ref, wfi_ref, wfo_ref, gf_ref,
                  y_ref, *, final_norm):
    d = x_ref.shape[-1]
    gate = jax.nn.sigmoid(mg_ref[0])
    mixed = gate[:, 0:d] * _dot(osb_ref[0], wsb_ref[...]) + gate[:, d:2 * d] * _dot(onsa_ref[0], wnsa_ref[...])
    x1 = x_ref[0] + _dot(mixed.astype(BF16), wout_ref[...])
    ms = jnp.mean(x1 * x1, axis=-1, keepdims=True)
    hn = (x1 * lax.rsqrt(ms + EPS) * g2_ref[...]).astype(BF16)
    u = jnp.maximum(_dot(hn, wfi_ref[...]), 0.0)
    x2 = x1 + _dot((u * u).astype(BF16), wfo_ref[...])
    if final_norm:
        ms2 = jnp.mean(x2 * x2, axis=-1, keepdims=True)
        x2 = x2 * lax.rsqrt(ms2 + EPS) * gf_ref[...]
    y_ref[0] = x2


def _merge(x, o_sb, o_nsa, mg, w_up_sb, w_up_nsa, w_out, norm2_g, w_ff_in, w_ff_out, norm_f_g, final_norm, tm):
    b, s, d = x.shape
    d_ff = w_ff_in.shape[1]
    row = lambda width: pl.BlockSpec((1, tm, width), lambda bi, i: (bi, i, 0))
    return pl.pallas_call(
        functools.partial(_merge_kernel, final_norm=final_norm),
        grid=(b, s // tm),
        in_specs=[row(d), row(SB_W), row(NSA_QW), row(2 * d),
                  _const_spec((SB_W, d)), _const_spec((NSA_QW, d)), _const_spec((d, d)), _const_spec((1, d)),
                  _const_spec((d, d_ff)), _const_spec((d_ff, d)), _const_spec((1, d))],
        out_specs=row(d),
        out_shape=jax.ShapeDtypeStruct((b, s, d), F32),
        compiler_params=_params("parallel", "parallel"),
        name="merge_mlp",
    )(x, o_sb, o_nsa, mg, w_up_sb, w_up_nsa, w_out, norm2_g.reshape(1, d), w_ff_in, w_ff_out, norm_f_g.reshape(1, d))


def _row_tile(n, target):
    t = min(n, target)
    assert n % t == 0
    return t


def _prompt_layer(x, wl, norm_f_g, final_norm):
    b, s, d = x.shape
    pr = _project(x, wl["norm1_g"], wl["w_main"], wl["w_t"], _row_tile(s, 256))
    kc, vct = _compress_prompt(pr["ck"], pr["cv"], wl["pool_k"], wl["pool_v"], wl["wbd_k"], wl["wbd_v"])
    o_sb = _sb_prompt(pr["sq"], pr["sktb"], pr["svtb"], _row_tile(s, 256))
    o_nsa = _nsa_prompt(pr["nq"], pr["lkb"], pr["lvtb"], pr["wkb"], pr["wvtb"], kc, vct, pr["ngt"], LANES, 1024)
    y = _merge(x, o_sb, o_nsa, pr["mg"], wl["w_up_sb"], wl["w_up_nsa"], wl["w_out"], wl["norm2_g"],
               wl["w_ff_in"], wl["w_ff_out"], norm_f_g, final_norm, _row_tile(s, 256))
    n_keep = min(WINDOW, s)
    state = (_token_major(pr["skt"], SB_HEADS), _token_major(pr["svt"], SB_HEADS),
             _token_major(pr["ckt"], NSA_GROUPS), _token_major(pr["cvt"], NSA_GROUPS),
             _token_major(pr["lkt"], NSA_GROUPS), _token_major(pr["lvt"], NSA_GROUPS),
             _token_major(pr["wkt"][:, :, s - n_keep:], NSA_GROUPS),
             _token_major(pr["wvt"][:, :, s - n_keep:], NSA_GROUPS))
    return y, state


def _sample_layer(x, layer, caches_t, n_pool, win_k, win_v, page_table, wl, norm_f_g, final_norm):
    c_sb_k, c_sb_v, c_cmp_k, c_cmp_v, c_slc_k, c_slc_v = caches_t
    db, n_new, d = x.shape
    assert n_new == 1
    n_pages = page_table.shape[1]
    page = c_sb_k.shape[-1]
    past = n_pages * page
    pt_flat = page_table.reshape(-1).astype(jnp.int32)
    pr = _project(x.reshape(1, db, d), wl["norm1_g"], wl["w_main"], wl["w_t"], db)
    n_pc = 16 if n_pages % 16 == 0 else 2
    o_sb = _sb_sample(pr["sq"][0], c_sb_k, c_sb_v, pt_flat, layer, n_pool, n_pages, n_pc)

    qn_s = pr["nq"][0].astype(F32).reshape(db, NSA_HEADS, LANES)
    n_slc = -(-(past + n_new) // SLC_BLOCK)
    cur = past // SLC_BLOCK
    n_slc_pad = -(-n_slc // LANES) * LANES
    o_cmp, imp = _nsa_cmp_sample(qn_s, c_cmp_k, c_cmp_v, pt_flat, layer, n_pool, n_pages, n_pc,
                                 wl["pool_k"], wl["pool_v"], wl["wbd_k"], wl["wbd_v"], n_slc_pad)
    n_rank = -(-n_slc // 8) * 8
    imp_t = imp[:, :NSA_GROUPS, :n_rank].reshape(db * NSA_GROUPS, n_rank).T
    n_sel = min(N_SEL, n_slc)
    idx = _topk_sample(imp_t, n_slc, cur, n_sel)
    idx_flat = idx.T.reshape(-1)

    new_t = lambda name: pr[name][0].T
    zeros = jnp.zeros((db, NSA_KVW), F32)
    new_rows = jnp.stack([new_t("lkt"), new_t("lvt"), new_t("wkt"), new_t("wvt"), zeros, zeros, zeros, zeros], axis=1)
    ng = pr["ngt"][0].reshape(NSA_GROUPS, NG_ROWS, db)[:, :HEADS_PER_GROUP * N_NSA_BRANCH]
    ng = ng.reshape(NSA_HEADS, N_NSA_BRANCH, db).transpose(2, 0, 1)
    ng_s = jnp.pad(ng, ((0, 0), (0, 0), (0, LANES - N_NSA_BRANCH)))
    win_kt = _channel_major(win_k)
    win_vt = _channel_major(win_v)
    o_all = _nsa_sel_sample(qn_s, c_slc_k, c_slc_v, win_kt, win_vt, new_rows, ng_s, o_cmp, pt_flat, idx_flat, layer,
                            n_pool, n_pages)
    o5 = o_all.reshape(db, NSA_GROUPS, HEADS_PER_GROUP, NSA_GROUPS, NSA_DIM)
    o_nsa = jnp.stack([o5[:, g, :, g, :] for g in range(NSA_GROUPS)], axis=1).reshape(1, db, NSA_QW)

    y = _merge(x.reshape(1, db, d), o_sb.reshape(1, db, SB_W).astype(BF16), o_nsa.astype(BF16), pr["mg"],
               wl["w_up_sb"], wl["w_up_nsa"], wl["w_out"], wl["norm2_g"], wl["w_ff_in"], wl["w_ff_out"],
               norm_f_g, final_norm, db)
    n_buf = win_k.shape[1]
    n_keep = min(WINDOW, past + n_new)
    new_state = lambda name, groups: _token_major(pr[name][0][None], groups)[0][:, None]
    wk_all = jnp.concatenate([win_k, new_state("wkt", NSA_GROUPS)], axis=1)
    wv_all = jnp.concatenate([win_v, new_state("wvt", NSA_GROUPS)], axis=1)
    state = (new_state("skt", SB_HEADS), new_state("svt", SB_HEADS),
             new_state("ckt", NSA_GROUPS), new_state("cvt", NSA_GROUPS),
             new_state("lkt", NSA_GROUPS), new_state("lvt", NSA_GROUPS),
             wk_all[:, n_buf + n_new - n_keep:], wv_all[:, n_buf + n_new - n_keep:])
    return y.reshape(db, n_new, d), state


def kernel(x_prompt, x_sample, cache_sb_k, cache_sb_v, cache_cmp_k, cache_cmp_v, cache_slc_k, cache_slc_v,
           state_win_k, state_win_v, page_table, norm1_g, w_in, cmp_pool_k, cmp_pool_v, cmp_proj_k, cmp_proj_v,
           w_up_sb, w_up_nsa, w_out, norm2_g, w_ff_in, w_ff_out, norm_f_g):
    depth = w_in.shape[0]
    d_model = x_prompt.shape[-1]
    n_pool = cache_sb_k.shape[1]
    h_p, h_s = x_prompt, x_sample
    pages = lambda c: _channel_major(c).reshape((depth * n_pool,) + (c.shape[3] * c.shape[4], c.shape[2]))
    caches_t = tuple(pages(c) for c in (cache_sb_k, cache_sb_v, cache_cmp_k, cache_cmp_v, cache_slc_k, cache_slc_v))
    new_p, new_s = [], []
    for l in range(depth):
        w_main, w_t = _pack_proj_weights(w_in[l], d_model)
        wl = dict(norm1_g=norm1_g[l], w_main=w_main, w_t=w_t,
                  pool_k=cmp_pool_k[l].reshape(CMP_LEN, NSA_KVW), pool_v=cmp_pool_v[l].reshape(CMP_LEN, NSA_KVW),
                  wbd_k=_block_diag(cmp_proj_k[l]), wbd_v=_block_diag(cmp_proj_v[l]),
                  w_up_sb=w_up_sb[l].astype(BF16), w_up_nsa=w_up_nsa[l].astype(BF16), w_out=w_out[l].astype(BF16),
                  norm2_g=norm2_g[l], w_ff_in=w_ff_in[l].astype(BF16), w_ff_out=w_ff_out[l].astype(BF16))
        last = l == depth - 1
        h_p, st_p = _prompt_layer(h_p, wl, norm_f_g, last)
        h_s, st_s = _sample_layer(h_s, l, caches_t, n_pool, state_win_k[l], state_win_v[l], page_table, wl,
                                  norm_f_g, last)
        new_p.append(st_p)
        new_s.append(st_s)
    stk = lambda lst, j: jnp.stack([st[j] for st in lst])
    return (h_p, h_s) + tuple(stk(new_p, j) for j in range(8)) + tuple(stk(new_s, j) for j in range(8))
```
